```python
import jax, jax.numpy as jnp
from jax import lax
import numpy as np

D_MODEL = 1024
BATCH = 8
SEQ = 4096
DEPTH = 4

CHUNK = 64
N_META = 16
CONV_WIDTH = 3
D_CONV = D_MODEL
N_RET_HEADS = 4
RET_QK_DIM = 256
RET_V_DIM = 512
D_RET_QK = N_RET_HEADS * RET_QK_DIM
D_RET_V = N_RET_HEADS * RET_V_DIM
ROPE_BASE = 10000.0
RMS_EPS = 1e-6
GN_EPS = 1e-5
D_IN = 4 * D_CONV + 2 * D_RET_QK + 2 * D_RET_V + 2 * D_MODEL

kernel_name = "hybrid_shortconv_retention_metatoken_trunk"


def _rmsnorm(x, g):
    xf = x.astype(jnp.float32)
    y = xf * lax.rsqrt(jnp.mean(xf * xf, axis=-1, keepdims=True) + RMS_EPS)
    return (y * g.astype(jnp.float32)).astype(x.dtype)


def _rope(x, pos):
    half = x.shape[-1] // 2
    inv = ROPE_BASE ** (-jnp.arange(half, dtype=jnp.float32) / half)
    ang = pos[:, None] * inv[None, :]
    cos = jnp.cos(ang)[None, :, None, :]
    sin = jnp.sin(ang)[None, :, None, :]
    x1, x2 = x[..., :half], x[..., half:]
    return jnp.concatenate([x1 * cos - x2 * sin, x1 * sin + x2 * cos], axis=-1)


def _short_conv_branch(h, gb, gc, z, conv_w, conv_b, w_out):
    u = gc * h
    up = jnp.pad(u, ((0, 0), (CONV_WIDTH - 1, 0), (0, 0)))
    T = u.shape[1]
    y = conv_b + sum(conv_w[j] * up[:, j:j + T] for j in range(CONV_WIDTH))
    y = gb * y * jax.nn.silu(z)
    return y @ w_out


def _retention_branch(q, k, v, z, gn_g, w_out):
    Bsz, T, _ = q.shape
    H, dk, dv = N_RET_HEADS, RET_QK_DIM, RET_V_DIM
    pos = jnp.arange(T, dtype=jnp.float32)
    qf = _rope(q.astype(jnp.float32).reshape(Bsz, T, H, dk), pos)
    kf = _rope(k.astype(jnp.float32).reshape(Bsz, T, H, dk), pos) * (dk ** -0.5)
    vf = v.astype(jnp.float32).reshape(Bsz, T, H, dv)
    n_pad = (-N_META) % CHUNK
    padw = ((0, 0), (n_pad, 0), (0, 0), (0, 0))
    qf, kf, vf = (jnp.pad(a, padw) for a in (qf, kf, vf))
    L = T + n_pad
    n_chunks = L // CHUNK

    def to_chunks(a):
        return a.reshape(Bsz, n_chunks, CHUNK, H, a.shape[-1]).transpose(1, 0, 3, 2, 4)

    qc, kc, vc = to_chunks(qf), to_chunks(kf), to_chunks(vf)
    log_g = jnp.log(1.0 - 2.0 ** (-5.0 - jnp.arange(H, dtype=jnp.float32)))
    idx = jnp.arange(CHUNK, dtype=jnp.float32)
    intra_dec = jnp.exp(log_g[:, None, None] * jnp.abs(idx[:, None] - idx[None, :]))
    q_dec = jnp.exp(log_g[:, None] * (idx + 1.0))[None, :, :, None]
    k_dec = jnp.exp(log_g[:, None] * (CHUNK - 1.0 - idx))[None, :, :, None]
    chunk_dec = jnp.exp(log_g * CHUNK)[None, :, None, None]

    def step(state, inp):
        qi, ki, vi = inp
        s = jnp.einsum('bhnd,bhmd->bhnm', qi, ki) * intra_dec
        intra = jnp.einsum('bhnm,bhmv->bhnv', s, vi)
        cross = jnp.einsum('bhnd,bhdv->bhnv', qi * q_dec, state)
        new_state = state * chunk_dec + jnp.einsum('bhmd,bhmv->bhdv', ki * k_dec, vi)
        return new_state, intra + cross

    state0 = jnp.zeros((Bsz, H, dk, dv), jnp.float32)
    _, out = lax.scan(step, state0, (qc, kc, vc))
    out = out.transpose(1, 0, 3, 2, 4).reshape(Bsz, L, H, dv)[:, n_pad:]
    mu = jnp.mean(out, axis=-1, keepdims=True)
    var = jnp.mean(jnp.square(out - mu), axis=-1, keepdims=True)
    out = (out - mu) * lax.rsqrt(var + GN_EPS) * gn_g.astype(jnp.float32).reshape(H, dv)
    out = out.reshape(Bsz, T, D_RET_V).astype(z.dtype) * jax.nn.silu(z)
    return out @ w_out


def setup_inputs(seed: int = 0) -> dict:
    key = jax.random.key(seed)
    ks = jax.random.split(key, 16)
    nrm = jax.random.normal
    f32 = jnp.float32
    return {
        "x": nrm(ks[0], (BATCH, SEQ, D_MODEL), f32),
        "meta": nrm(ks[1], (N_META, D_MODEL), f32),
        "pre_norm_g": 1.0 + 0.05 * nrm(ks[2], (DEPTH, D_MODEL), f32),
        "w_in": nrm(ks[3], (DEPTH, D_MODEL, D_IN), f32) * D_MODEL ** -0.5,
        "b_in": 0.02 * nrm(ks[4], (DEPTH, D_IN), f32),
        "conv_w": nrm(ks[5], (DEPTH, CONV_WIDTH, D_CONV), f32) * CONV_WIDTH ** -0.5,
        "conv_b": 0.02 * nrm(ks[6], (DEPTH, D_CONV), f32),
        "w_conv_out": nrm(ks[7], (DEPTH, D_CONV, D_MODEL), f32) * D_CONV ** -0.5,
        "ret_gn_g": 1.0 + 0.05 * nrm(ks[8], (DEPTH, D_RET_V), f32),
        "w_ret_out": nrm(ks[9], (DEPTH, D_RET_V, D_MODEL), f32) * D_RET_V ** -0.5,
        "w_o": nrm(ks[10], (DEPTH, D_MODEL, D_MODEL), f32) * D_MODEL ** -0.5,
        "post_norm_g": 1.0 + 0.05 * nrm(ks[11], (DEPTH, D_MODEL), f32),
    }


def reference(x, meta, pre_norm_g, w_in, b_in, conv_w, conv_b, w_conv_out,
              ret_gn_g, w_ret_out, w_o, post_norm_g):
    Bsz = x.shape[0]
    meta_b = jnp.broadcast_to(meta[None].astype(x.dtype), (Bsz, N_META, D_MODEL))
    h = jnp.concatenate([meta_b, x], axis=1)
    sizes = (D_CONV,) * 4 + (D_RET_QK,) * 2 + (D_RET_V,) * 2 + (D_MODEL,) * 2
    split_at = [int(s) for s in np.cumsum(sizes)[:-1]]
    for l in range(DEPTH):
        xn = _rmsnorm(h, pre_norm_g[l])
        proj = xn @ w_in[l] + b_in[l]
        c_h, c_b, c_c, c_z, r_q, r_k, r_v, r_z, g_a, g_b = jnp.split(proj, split_at, axis=-1)
        y_a = _short_conv_branch(c_h, c_b, c_c, c_z, conv_w[l], conv_b[l], w_conv_out[l])
        y_b = _retention_branch(r_q, r_k, r_v, r_z, ret_gn_g[l], w_ret_out[l])
        y = jax.nn.sigmoid(g_a) * y_a + jax.nn.sigmoid(g_b) * y_b
        y = y @ w_o[l]
        h = h + _rmsnorm(y, post_norm_g[l])
    return h[:, N_META:]
```

```python
import functools

import jax
import jax.numpy as jnp
from jax import lax
from jax.experimental import pallas as pl
from jax.experimental.pallas import tpu as pltpu

D_MODEL = 1024
DEPTH = 4
CHUNK = 64
N_META = 16
CONV_WIDTH = 3
D_CONV = D_MODEL
N_HEADS = 4
DK = 256
DV = 512
D_QK = N_HEADS * DK
D_V = N_HEADS * DV
ROPE_BASE = 10000.0
RMS_EPS = 1e-6
GN_EPS = 1e-5
D_IN = 4 * D_CONV + 2 * D_QK + 2 * D_V + 2 * D_MODEL

OFF_CH = 0
OFF_CB = OFF_CH + D_CONV
OFF_CC = OFF_CB + D_CONV
OFF_CZ = OFF_CC + D_CONV
OFF_Q = OFF_CZ + D_CONV
OFF_K = OFF_Q + D_QK
OFF_V = OFF_K + D_QK
OFF_RZ = OFF_V + D_V
OFF_GA = OFF_RZ + D_V
OFF_GB = OFF_GA + D_MODEL

HALF = DK // 2
SUPER = 4 * CHUNK
TOKEN_TILE = SUPER
META_ROWS = 128
SUBLANES = 8
VMEM_LIMIT_BYTES = 56 * 1024 * 1024

F32 = jnp.float32
BF16 = jnp.bfloat16


def _gammas():
    return [1.0 - 2.0 ** (-5.0 - h) for h in range(N_HEADS)]


def _dot(a, b):
    return jnp.dot(a, b, preferred_element_type=F32)


def _dot_nt(a, b):
    return lax.dot_general(a, b, (((1,), (1,)), ((), ())), preferred_element_type=F32)


def _dot_tn(a, b):
    return lax.dot_general(a, b, (((0,), (0,)), ((), ())), preferred_element_type=F32)


def _rmsnorm(x, g):
    ms = jnp.mean(x * x, axis=-1, keepdims=True)
    return x * lax.rsqrt(ms + RMS_EPS) * g


def _silu(x):
    return x * jax.nn.sigmoid(x)


def _rope(x, cos, sin):
    x1, x2 = x[:, :HALF], x[:, HALF:]
    return x1 * cos - x2 * sin, x1 * sin + x2 * cos


def _layer_rows(h, cos, sin, p, u_ref, retention):
    rows = h.shape[0]
    xn = _rmsnorm(h, p["pre_g"][...]).astype(BF16)

    def proj(off, width):
        return _dot(xn, p["w_in"][:, off:off + width]) + p["b_in"][:, off:off + width]

    u = proj(OFF_CC, D_CONV) * proj(OFF_CH, D_CONV)
    u_ref[SUBLANES:SUBLANES + rows, :] = u
    conv = p["conv_b"][...] + p["conv_w"][2:3, :] * u
    conv = conv + p["conv_w"][1:2, :] * u_ref[SUBLANES - 1:SUBLANES - 1 + rows, :]
    conv = conv + p["conv_w"][0:1, :] * u_ref[SUBLANES - 2:SUBLANES - 2 + rows, :]
    y = proj(OFF_CB, D_CONV) * conv * _silu(proj(OFF_CZ, D_CONV))
    mix = jax.nn.sigmoid(proj(OFF_GA, D_MODEL)) * _dot(y.astype(BF16), p["w_co"][...])

    y_b = jnp.zeros((rows, D_MODEL), F32)
    for hd in range(N_HEADS):
        q = _rope(proj(OFF_Q + hd * DK, DK), cos, sin)
        k = _rope(proj(OFF_K + hd * DK, DK), cos, sin)
        v = proj(OFF_V + hd * DV, DV).astype(BF16)
        o = retention(hd, q, k, v)
        mu = jnp.mean(o, axis=-1, keepdims=True)
        oc = o - mu
        var = jnp.mean(oc * oc, axis=-1, keepdims=True)
        on = oc * lax.rsqrt(var + GN_EPS) * p["gn_g"][:, hd * DV:(hd + 1) * DV]
        on = on * _silu(proj(OFF_RZ + hd * DV, DV))
        y_b = y_b + _dot(on.astype(BF16), p["w_ro"][hd * DV:(hd + 1) * DV, :])

    mix = mix + jax.nn.sigmoid(proj(OFF_GB, D_MODEL)) * y_b
    y_o = _dot(mix.astype(BF16), p["w_o"][...])
    return h + _rmsnorm(y_o, p["post_g"][...])


def _cat(halves):
    return jnp.concatenate(halves, axis=1)


def _main_kernel(h_ref, cos_ref, sin_ref, dmat_ref, qdec_ref, kdec_ref, s_init_ref, utail_ref,
                 pre_g, b_in, conv_w, conv_b, gn_g, post_g, w_in, w_co, w_ro, w_o,
                 out_ref, state_ref, u_ref):
    rows = h_ref.shape[0]

    @pl.when(pl.program_id(1) == 0)
    def _():
        state_ref[...] = s_init_ref[...]
        u_ref[0:SUBLANES, :] = utail_ref[...]

    gammas = _gammas()

    def retention(hd, q, k, v):
        outs = []
        for c in range(rows // SUPER):
            r = slice(c * SUPER, (c + 1) * SUPER)
            q1, q2 = q[0][r], q[1][r]
            k1, k2 = k[0][r], k[1][r]
            vc = v[r]
            qdec = qdec_ref[hd]
            kdec = kdec_ref[hd]
            qr = _cat([q1, q2]).astype(BF16)
            kr = _cat([k1, k2]).astype(BF16)
            qd = _cat([q1 * qdec, q2 * qdec]).astype(BF16)
            kd = _cat([k1 * kdec, k2 * kdec]).astype(BF16)
            s = (_dot_nt(qr, kr) * dmat_ref[hd]).astype(BF16)
            state = state_ref[hd]
            o = _dot(s, vc) + _dot(qd, state.astype(BF16))
            state_ref[hd] = (gammas[hd] ** SUPER) * state + _dot_tn(kd, vc)
            outs.append(o)
        return outs[0] if len(outs) == 1 else jnp.concatenate(outs, axis=0)

    p = dict(pre_g=pre_g, b_in=b_in, conv_w=conv_w, conv_b=conv_b, gn_g=gn_g, post_g=post_g,
             w_in=w_in, w_co=w_co, w_ro=w_ro, w_o=w_o)
    out_ref[...] = _layer_rows(h_ref[...], cos_ref[...], sin_ref[...], p, u_ref, retention)
    u_ref[0:SUBLANES, :] = u_ref[rows:rows + SUBLANES, :]


def _meta_kernel(meta_ref, cos_ref, sin_ref, dmat_ref, kdec_ref,
                 pre_g, b_in, conv_w, conv_b, gn_g, post_g, w_in, w_co, w_ro, w_o,
                 s_out_ref, utail_ref, hm_ref, u_ref):
    @pl.when(pl.program_id(0) == 0)
    def _():
        hm_ref[...] = meta_ref[...]
        u_ref[0:SUBLANES, :] = jnp.zeros((SUBLANES, D_MODEL), F32)

    def retention(hd, q, k, v):
        kdec = kdec_ref[hd]
        qr = _cat(q).astype(BF16)
        kr = _cat(k).astype(BF16)
        kd = _cat([k[0] * kdec, k[1] * kdec]).astype(BF16)
        s = (_dot_nt(qr, kr) * dmat_ref[hd]).astype(BF16)
        s_out_ref[hd] = _dot_tn(kd, v)
        return _dot(s, v)

    p = dict(pre_g=pre_g, b_in=b_in, conv_w=conv_w, conv_b=conv_b, gn_g=gn_g, post_g=post_g,
             w_in=w_in, w_co=w_co, w_ro=w_ro, w_o=w_o)
    hm_ref[...] = _layer_rows(hm_ref[...], cos_ref[...], sin_ref[...], p, u_ref, retention)
    utail_ref[...] = u_ref[N_META:N_META + SUBLANES, :]


def _tables(seq):
    total = N_META + seq
    pos = jnp.arange(total, dtype=F32)
    inv = ROPE_BASE ** (-jnp.arange(HALF, dtype=F32) / HALF)
    ang = pos[:, None] * inv[None, :]
    cos, sin = jnp.cos(ang), jnp.sin(ang)

    log_g = jnp.log(1.0 - 2.0 ** (-5.0 - jnp.arange(N_HEADS, dtype=F32)))[:, None, None]
    scale = DK ** -0.5
    i = jnp.arange(SUPER, dtype=F32)
    dist = jnp.abs(i[:, None] - i[None, :])
    visible = (jnp.arange(SUPER)[None, :] // CHUNK) <= (jnp.arange(SUPER)[:, None] // CHUNK)
    dmat = jnp.where(visible[None], jnp.exp(log_g * dist[None]), 0.0) * scale
    lane = jnp.ones((1, 1, HALF), F32)
    qdec = jnp.exp(log_g * (i[None, :, None] + 1.0)) * lane
    kdec = jnp.exp(log_g * (SUPER - 1.0 - i[None, :, None])) * scale * lane

    m = jnp.arange(META_ROWS, dtype=F32)
    is_meta = jnp.arange(META_ROWS) < N_META
    mdist = jnp.abs(m[:, None] - m[None, :])
    dmat_m = jnp.where((is_meta[:, None] & is_meta[None, :])[None], jnp.exp(log_g * mdist[None]), 0.0) * scale
    kdec_m = jnp.where(is_meta[None, :, None], jnp.exp(log_g * (N_META - 1.0 - m[None, :, None])), 0.0)
    kdec_m = kdec_m * scale * lane
    return dict(cos=cos, sin=sin, dmat=dmat, qdec=qdec, kdec=kdec, dmat_m=dmat_m, kdec_m=kdec_m)


def _resident(tail_shape, index_map):
    return pl.BlockSpec((None,) + tail_shape, index_map, pipeline_mode=pl.Buffered(1))


def _param_specs(params, layer_index_map):
    return [_resident(a.shape[1:], functools.partial(layer_index_map, nd=a.ndim - 1)) for a in params]


def _meta_layer_map(l, nd):
    return (l,) + (0,) * nd


def _main_layer_map(b, t, layer, nd):
    return (layer,) + (0,) * nd


def _const_map2(*_):
    return (0, 0)


def _const_map3(*_):
    return (0, 0, 0)


def _meta_call(meta_pad, tabs, params):
    in_specs = [
        pl.BlockSpec((META_ROWS, D_MODEL), _const_map2),
        pl.BlockSpec((META_ROWS, HALF), _const_map2),
        pl.BlockSpec((META_ROWS, HALF), _const_map2),
        pl.BlockSpec((N_HEADS, META_ROWS, META_ROWS), _const_map3),
        pl.BlockSpec((N_HEADS, META_ROWS, HALF), _const_map3),
    ] + _param_specs(params, _meta_layer_map)
    out_shape = (jax.ShapeDtypeStruct((DEPTH, N_HEADS, DK, DV), F32),
                 jax.ShapeDtypeStruct((DEPTH, SUBLANES, D_MODEL), F32))
    out_specs = (pl.BlockSpec((None, N_HEADS, DK, DV), lambda l: (l, 0, 0, 0)),
                 pl.BlockSpec((None, SUBLANES, D_MODEL), lambda l: (l, 0, 0)))
    return pl.pallas_call(
        _meta_kernel,
        grid=(DEPTH,),
        in_specs=in_specs,
        out_specs=out_specs,
        out_shape=out_shape,
        scratch_shapes=[pltpu.VMEM((META_ROWS, D_MODEL), F32),
                        pltpu.VMEM((SUBLANES + META_ROWS, D_MODEL), F32)],
        compiler_params=pltpu.CompilerParams(dimension_semantics=("arbitrary",),
                                             vmem_limit_bytes=VMEM_LIMIT_BYTES),
        name="meta_tokens",
    )(meta_pad, tabs["cos"][:META_ROWS], tabs["sin"][:META_ROWS], tabs["dmat_m"], tabs["kdec_m"], *params)


def _main_call(h, layer, tabs, s_init, utail, params):
    batch, seq, _ = h.shape
    layer_map = functools.partial(_main_layer_map, layer=layer)
    in_specs = [
        pl.BlockSpec((None, TOKEN_TILE, D_MODEL), lambda b, t: (b, t, 0)),
        pl.BlockSpec((TOKEN_TILE, HALF), lambda b, t: (t, 0)),
        pl.BlockSpec((TOKEN_TILE, HALF), lambda b, t: (t, 0)),
        pl.BlockSpec((N_HEADS, SUPER, SUPER), _const_map3, pipeline_mode=pl.Buffered(1)),
        pl.BlockSpec((N_HEADS, SUPER, HALF), _const_map3, pipeline_mode=pl.Buffered(1)),
        pl.BlockSpec((N_HEADS, SUPER, HALF), _const_map3, pipeline_mode=pl.Buffered(1)),
        _resident(s_init.shape[1:], functools.partial(layer_map, nd=3)),
        _resident(utail.shape[1:], functools.partial(layer_map, nd=2)),
    ] + _param_specs(params, layer_map)
    return pl.pallas_call(
        _main_kernel,
        grid=(batch, seq // TOKEN_TILE),
        in_specs=in_specs,
        out_specs=pl.BlockSpec((None, TOKEN_TILE, D_MODEL), lambda b, t: (b, t, 0)),
        out_shape=jax.ShapeDtypeStruct(h.shape, F32),
        scratch_shapes=[pltpu.VMEM((N_HEADS, DK, DV), F32),
                        pltpu.VMEM((SUBLANES + TOKEN_TILE, D_MODEL), F32)],
        compiler_params=pltpu.CompilerParams(dimension_semantics=("arbitrary", "arbitrary"),
                                             vmem_limit_bytes=VMEM_LIMIT_BYTES),
        name="mixer_layer",
    )(h, tabs["cos"][N_META:], tabs["sin"][N_META:], tabs["dmat"], tabs["qdec"], tabs["kdec"],
      s_init, utail, *params)


def kernel(x, meta, pre_norm_g, w_in, b_in, conv_w, conv_b, w_conv_out, ret_gn_g, w_ret_out, w_o, post_norm_g):
    batch, seq, d_model = x.shape
    assert d_model == D_MODEL and seq % TOKEN_TILE == 0 and meta.shape == (N_META, D_MODEL)
    assert w_in.shape == (DEPTH, D_MODEL, D_IN)
    tabs = _tables(seq)
    params = (
        pre_norm_g.reshape(DEPTH, 1, D_MODEL),
        b_in.reshape(DEPTH, 1, D_IN),
        conv_w,
        conv_b.reshape(DEPTH, 1, D_CONV),
        ret_gn_g.reshape(DEPTH, 1, D_V),
        post_norm_g.reshape(DEPTH, 1, D_MODEL),
        w_in.astype(BF16),
        w_conv_out.astype(BF16),
        w_ret_out.astype(BF16),
        w_o.astype(BF16),
    )
    meta_pad = jnp.zeros((META_ROWS, D_MODEL), F32).at[:N_META].set(meta.astype(F32))
    s_init, utail = _meta_call(meta_pad, tabs, params)
    h = x
    for layer in range(DEPTH):
        h = _main_call(h, layer, tabs, s_init, utail, params)
    return h
```

```python
import functools

import jax
import jax.numpy as jnp
from jax import lax
from jax.experimental import pallas as pl
from jax.experimental.pallas import tpu as pltpu

D_MODEL = 1024
DEPTH = 4
CHUNK = 64
N_META = 16
CONV_WIDTH = 3
D_CONV = D_MODEL
N_HEADS = 4
DK = 256
DV = 512
D_QK = N_HEADS * DK
D_V = N_HEADS * DV
ROPE_BASE = 10000.0
RMS_EPS = 1e-6
GN_EPS = 1e-5
D_IN = 4 * D_CONV + 2 * D_QK + 2 * D_V + 2 * D_MODEL

OFF_CH = 0
OFF_CB = OFF_CH + D_CONV
OFF_CC = OFF_CB + D_CONV
OFF_CZ = OFF_CC + D_CONV
OFF_Q = OFF_CZ + D_CONV
OFF_K = OFF_Q + D_QK
OFF_V = OFF_K + D_QK
OFF_RZ = OFF_V + D_V
OFF_GA = OFF_RZ + D_V
OFF_GB = OFF_GA + D_MODEL

HALF = DK // 2
SUPER = 4 * CHUNK
TOKEN_TILE = 2 * SUPER
META_ROWS = 128
SUBLANES = 8
VMEM_LIMIT_BYTES = 60 * 1024 * 1024

F32 = jnp.float32
BF16 = jnp.bfloat16


def _gammas():
    return [1.0 - 2.0 ** (-5.0 - h) for h in range(N_HEADS)]


def _dot(a, b):
    return jnp.dot(a, b, preferred_element_type=F32)


def _dot_nt(a, b):
    return lax.dot_general(a, b, (((1,), (1,)), ((), ())), preferred_element_type=F32)


def _dot_tn(a, b):
    return lax.dot_general(a, b, (((0,), (0,)), ((), ())), preferred_element_type=F32)


def _rmsnorm(x, g):
    ms = jnp.mean(x * x, axis=-1, keepdims=True)
    return x * lax.rsqrt(ms + RMS_EPS) * g


def _silu(x):
    return x * jax.nn.sigmoid(x)


def _rope(x, cos, sin):
    x1, x2 = x[:, :HALF], x[:, HALF:]
    return x1 * cos - x2 * sin, x1 * sin + x2 * cos


def _layer_rows(h, cos, sin, p, u_ref, retention):
    rows = h.shape[0]
    xn = _rmsnorm(h, p["pre_g"][...]).astype(BF16)

    def proj(off, width):
        return _dot(xn, p["w_in"][:, off:off + width]) + p["b_in"][:, off:off + width]

    u = proj(OFF_CC, D_CONV) * proj(OFF_CH, D_CONV)
    u_ref[SUBLANES:SUBLANES + rows, :] = u
    conv = p["conv_b"][...] + p["conv_w"][2:3, :] * u
    conv = conv + p["conv_w"][1:2, :] * u_ref[SUBLANES - 1:SUBLANES - 1 + rows, :]
    conv = conv + p["conv_w"][0:1, :] * u_ref[SUBLANES - 2:SUBLANES - 2 + rows, :]
    y = proj(OFF_CB, D_CONV) * conv * _silu(proj(OFF_CZ, D_CONV))
    mix = jax.nn.sigmoid(proj(OFF_GA, D_MODEL)) * _dot(y.astype(BF16), p["w_co"][...])

    y_b = jnp.zeros((rows, D_MODEL), F32)
    for hd in range(N_HEADS):
        q = _rope(proj(OFF_Q + hd * DK, DK), cos, sin)
        k = _rope(proj(OFF_K + hd * DK, DK), cos, sin)
        v = proj(OFF_V + hd * DV, DV).astype(BF16)
        o = retention(hd, q, k, v)
        mu = jnp.mean(o, axis=-1, keepdims=True)
        oc = o - mu
        var = jnp.mean(oc * oc, axis=-1, keepdims=True)
        on = oc * lax.rsqrt(var + GN_EPS) * p["gn_g"][:, hd * DV:(hd + 1) * DV]
        on = on * _silu(proj(OFF_RZ + hd * DV, DV))
        y_b = y_b + _dot(on.astype(BF16), p["w_ro"][hd * DV:(hd + 1) * DV, :])

    mix = mix + jax.nn.sigmoid(proj(OFF_GB, D_MODEL)) * y_b
    y_o = _dot(mix.astype(BF16), p["w_o"][...])
    return h + _rmsnorm(y_o, p["post_g"][...])


def _cat(halves):
    return jnp.concatenate(halves, axis=1)


def _main_kernel(h_ref, cos_ref, sin_ref, dmat_ref, qdec_ref, kdec_ref, s_init_ref, utail_ref,
                 pre_g, b_in, conv_w, conv_b, gn_g, post_g, w_in, w_co, w_ro, w_o,
                 out_ref, state_ref, u_ref):
    rows = h_ref.shape[0]

    @pl.when(pl.program_id(1) == 0)
    def _():
        state_ref[...] = s_init_ref[...]
        u_ref[0:SUBLANES, :] = utail_ref[...]

    gammas = _gammas()

    def retention(hd, q, k, v):
        outs = []
        for c in range(rows // SUPER):
            r = slice(c * SUPER, (c + 1) * SUPER)
            q1, q2 = q[0][r], q[1][r]
            k1, k2 = k[0][r], k[1][r]
            vc = v[r]
            qdec = qdec_ref[hd]
            kdec = kdec_ref[hd]
            qr = _cat([q1, q2]).astype(BF16)
            kr = _cat([k1, k2]).astype(BF16)
            qd = _cat([q1 * qdec, q2 * qdec]).astype(BF16)
            kd = _cat([k1 * kdec, k2 * kdec]).astype(BF16)
            s = (_dot_nt(qr, kr) * dmat_ref[hd]).astype(BF16)
            state = state_ref[hd]
            o = _dot(s, vc) + _dot(qd, state.astype(BF16))
            state_ref[hd] = (gammas[hd] ** SUPER) * state + _dot_tn(kd, vc)
            outs.append(o)
        return outs[0] if len(outs) == 1 else jnp.concatenate(outs, axis=0)

    p = dict(pre_g=pre_g, b_in=b_in, conv_w=conv_w, conv_b=conv_b, gn_g=gn_g, post_g=post_g,
             w_in=w_in, w_co=w_co, w_ro=w_ro, w_o=w_o)
    out_ref[...] = _layer_rows(h_ref[...], cos_ref[...], sin_ref[...], p, u_ref, retention)
    u_ref[0:SUBLANES, :] = u_ref[rows:rows + SUBLANES, :]


def _meta_kernel(meta_ref, cos_ref, sin_ref, dmat_ref, kdec_ref,
                 pre_g, b_in, conv_w, conv_b, gn_g, post_g, w_in, w_co, w_ro, w_o,
                 s_out_ref, utail_ref, hm_ref, u_ref):
    @pl.when(pl.program_id(0) == 0)
    def _():
        hm_ref[...] = meta_ref[...]
        u_ref[0:SUBLANES, :] = jnp.zeros((SUBLANES, D_MODEL), F32)

    def retention(hd, q, k, v):
        kdec = kdec_ref[hd]
        qr = _cat(q).astype(BF16)
        kr = _cat(k).astype(BF16)
        kd = _cat([k[0] * kdec, k[1] * kdec]).astype(BF16)
        s = (_dot_nt(qr, kr) * dmat_ref[hd]).astype(BF16)
        s_out_ref[hd] = _dot_tn(kd, v)
        return _dot(s, v)

    p = dict(pre_g=pre_g, b_in=b_in, conv_w=conv_w, conv_b=conv_b, gn_g=gn_g, post_g=post_g,
             w_in=w_in, w_co=w_co, w_ro=w_ro, w_o=w_o)
    hm_ref[...] = _layer_rows(hm_ref[...], cos_ref[...], sin_ref[...], p, u_ref, retention)
    utail_ref[...] = u_ref[N_META:N_META + SUBLANES, :]


def _tables(seq):
    total = N_META + seq
    pos = jnp.arange(total, dtype=F32)
    inv = ROPE_BASE ** (-jnp.arange(HALF, dtype=F32) / HALF)
    ang = pos[:, None] * inv[None, :]
    cos, sin = jnp.cos(ang), jnp.sin(ang)

    log_g = jnp.log(1.0 - 2.0 ** (-5.0 - jnp.arange(N_HEADS, dtype=F32)))[:, None, None]
    scale = DK ** -0.5
    i = jnp.arange(SUPER, dtype=F32)
    dist = jnp.abs(i[:, None] - i[None, :])
    visible = (jnp.arange(SUPER)[None, :] // CHUNK) <= (jnp.arange(SUPER)[:, None] // CHUNK)
    dmat = jnp.where(visible[None], jnp.exp(log_g * dist[None]), 0.0) * scale
    lane = jnp.ones((1, 1, HALF), F32)
    qdec = jnp.exp(log_g * (i[None, :, None] + 1.0)) * lane
    kdec = jnp.exp(log_g * (SUPER - 1.0 - i[None, :, None])) * scale * lane

    m = jnp.arange(META_ROWS, dtype=F32)
    is_meta = jnp.arange(META_ROWS) < N_META
    mdist = jnp.abs(m[:, None] - m[None, :])
    dmat_m = jnp.where((is_meta[:, None] & is_meta[None, :])[None], jnp.exp(log_g * mdist[None]), 0.0) * scale
    kdec_m = jnp.where(is_meta[None, :, None], jnp.exp(log_g * (N_META - 1.0 - m[None, :, None])), 0.0)
    kdec_m = kdec_m * scale * lane
    return dict(cos=cos, sin=sin, dmat=dmat, qdec=qdec, kdec=kdec, dmat_m=dmat_m, kdec_m=kdec_m)


def _resident(tail_shape, index_map):
    return pl.BlockSpec((None,) + tail_shape, index_map, pipeline_mode=pl.Buffered(1))


def _param_specs(params, layer_index_map):
    return [_resident(a.shape[1:], functools.partial(layer_index_map, nd=a.ndim - 1)) for a in params]


def _meta_layer_map(l, nd):
    return (l,) + (0,) * nd


def _main_layer_map(b, t, layer, nd):
    return (layer,) + (0,) * nd


def _const_map2(*_):
    return (0, 0)


def _const_map3(*_):
    return (0, 0, 0)


def _meta_call(meta_pad, tabs, params):
    in_specs = [
        pl.BlockSpec((META_ROWS, D_MODEL), _const_map2),
        pl.BlockSpec((META_ROWS, HALF), _const_map2),
        pl.BlockSpec((META_ROWS, HALF), _const_map2),
        pl.BlockSpec((N_HEADS, META_ROWS, META_ROWS), _const_map3),
        pl.BlockSpec((N_HEADS, META_ROWS, HALF), _const_map3),
    ] + _param_specs(params, _meta_layer_map)
    out_shape = (jax.ShapeDtypeStruct((DEPTH, N_HEADS, DK, DV), F32),
                 jax.ShapeDtypeStruct((DEPTH, SUBLANES, D_MODEL), F32))
    out_specs = (pl.BlockSpec((None, N_HEADS, DK, DV), lambda l: (l, 0, 0, 0)),
                 pl.BlockSpec((None, SUBLANES, D_MODEL), lambda l: (l, 0, 0)))
    return pl.pallas_call(
        _meta_kernel,
        grid=(DEPTH,),
        in_specs=in_specs,
        out_specs=out_specs,
        out_shape=out_shape,
        scratch_shapes=[pltpu.VMEM((META_ROWS, D_MODEL), F32),
                        pltpu.VMEM((SUBLANES + META_ROWS, D_MODEL), F32)],
        compiler_params=pltpu.CompilerParams(dimension_semantics=("arbitrary",),
                                             vmem_limit_bytes=VMEM_LIMIT_BYTES),
        name="meta_tokens",
    )(meta_pad, tabs["cos"][:META_ROWS], tabs["sin"][:META_ROWS], tabs["dmat_m"], tabs["kdec_m"], *params)


def _main_call(h, layer, tabs, s_init, utail, params):
    batch, seq, _ = h.shape
    layer_map = functools.partial(_main_layer_map, layer=layer)
    in_specs = [
        pl.BlockSpec((None, TOKEN_TILE, D_MODEL), lambda b, t: (b, t, 0)),
        pl.BlockSpec((TOKEN_TILE, HALF), lambda b, t: (t, 0)),
        pl.BlockSpec((TOKEN_TILE, HALF), lambda b, t: (t, 0)),
        pl.BlockSpec((N_HEADS, SUPER, SUPER), _const_map3, pipeline_mode=pl.Buffered(1)),
        pl.BlockSpec((N_HEADS, SUPER, HALF), _const_map3, pipeline_mode=pl.Buffered(1)),
        pl.BlockSpec((N_HEADS, SUPER, HALF), _const_map3, pipeline_mode=pl.Buffered(1)),
        _resident(s_init.shape[1:], functools.partial(layer_map, nd=3)),
        _resident(utail.shape[1:], functools.partial(layer_map, nd=2)),
    ] + _param_specs(params, layer_map)
    return pl.pallas_call(
        _main_kernel,
        grid=(batch, seq // TOKEN_TILE),
        in_specs=in_specs,
        out_specs=pl.BlockSpec((None, TOKEN_TILE, D_MODEL), lambda b, t: (b, t, 0)),
        out_shape=jax.ShapeDtypeStruct(h.shape, F32),
        scratch_shapes=[pltpu.VMEM((N_HEADS, DK, DV), F32),
                        pltpu.VMEM((SUBLANES + TOKEN_TILE, D_MODEL), F32)],
        compiler_params=pltpu.CompilerParams(dimension_semantics=("arbitrary", "arbitrary"),
                                             vmem_limit_bytes=VMEM_LIMIT_BYTES),
        name="mixer_layer",
    )(h, tabs["cos"][N_META:], tabs["sin"][N_META:], tabs["dmat"], tabs["qdec"], tabs["kdec"],
      s_init, utail, *params)


def kernel(x, meta, pre_norm_g, w_in, b_in, conv_w, conv_b, w_conv_out, ret_gn_g, w_ret_out, w_o, post_norm_g):
    batch, seq, d_model = x.shape
    assert d_model == D_MODEL and seq % TOKEN_TILE == 0 and meta.shape == (N_META, D_MODEL)
    assert w_in.shape == (DEPTH, D_MODEL, D_IN)
    tabs = _tables(seq)
    params = (
        pre_norm_g.reshape(DEPTH, 1, D_MODEL),
        b_in.reshape(DEPTH, 1, D_IN),
        conv_w,
        conv_b.reshape(DEPTH, 1, D_CONV),
        ret_gn_g.reshape(DEPTH, 1, D_V),
        post_norm_g.reshape(DEPTH, 1, D_MODEL),
        w_in.astype(BF16),
        w_conv_out.astype(BF16),
        w_ret_out.astype(BF16),
        w_o.astype(BF16),
    )
    meta_pad = jnp.zeros((META_ROWS, D_MODEL), F32).at[:N_META].set(meta.astype(F32))
    s_init, utail = _meta_call(meta_pad, tabs, params)
    h = x
    for layer in range(DEPTH):
        h = _main_call(h, layer, tabs, s_init, utail, params)
    return h
```

```python
import functools

import jax
import jax.numpy as jnp
from jax import lax
from jax.experimental import pallas as pl
from jax.experimental.pallas import tpu as pltpu

D_MODEL = 1024
DEPTH = 4
CHUNK = 64
N_META = 16
CONV_WIDTH = 3
D_CONV = D_MODEL
N_HEADS = 4
DK = 256
DV = 512
D_QK = N_HEADS * DK
D_V = N_HEADS * DV
ROPE_BASE = 10000.0
RMS_EPS = 1e-6
GN_EPS = 1e-5
D_IN = 4 * D_CONV + 2 * D_QK + 2 * D_V + 2 * D_MODEL

OFF_CH = 0
OFF_CB = OFF_CH + D_CONV
OFF_CC = OFF_CB + D_CONV
OFF_CZ = OFF_CC + D_CONV
OFF_Q = OFF_CZ + D_CONV
OFF_K = OFF_Q + D_QK
OFF_V = OFF_K + D_QK
OFF_RZ = OFF_V + D_V
OFF_GA = OFF_RZ + D_V
OFF_GB = OFF_GA + D_MODEL

HALF = DK // 2
SUPER = 4 * CHUNK
TOKEN_TILE = SUPER
META_ROWS = 128
SUBLANES = 8
VMEM_LIMIT_BYTES = 60 * 1024 * 1024

F32 = jnp.float32
BF16 = jnp.bfloat16


def _gammas():
    return [1.0 - 2.0 ** (-5.0 - h) for h in range(N_HEADS)]


ROW_PACK = 2


def _pack_rows(w):
    wb = w.astype(BF16)
    pairs = jnp.stack([wb[..., 0::ROW_PACK, :], wb[..., 1::ROW_PACK, :]], axis=-1)
    return lax.bitcast_convert_type(pairs, jnp.uint32)


def _weight(w_ref, rows, cols):
    packed = w_ref[rows.start // ROW_PACK:rows.stop // ROW_PACK, cols]
    return pltpu.bitcast(packed, BF16)


def _dot(a, b):
    return jnp.dot(a, b, preferred_element_type=F32)


def _dot_nt(a, b):
    return lax.dot_general(a, b, (((1,), (1,)), ((), ())), preferred_element_type=F32)


def _dot_tn(a, b):
    return lax.dot_general(a, b, (((0,), (0,)), ((), ())), preferred_element_type=F32)


def _rmsnorm(x, g):
    ms = jnp.mean(x * x, axis=-1, keepdims=True)
    return x * lax.rsqrt(ms + RMS_EPS) * g


def _silu(x):
    return x * jax.nn.sigmoid(x)


def _rope(x, cos, sin):
    x1, x2 = x[:, :HALF], x[:, HALF:]
    return x1 * cos - x2 * sin, x1 * sin + x2 * cos


def _layer_rows(h, cos, sin, p, u_ref, retention):
    rows = h.shape[0]
    xn = _rmsnorm(h, p["pre_g"][...]).astype(BF16)

    def proj(off, width):
        w = _weight(p["w_in"], slice(0, D_MODEL), slice(off, off + width))
        return _dot(xn, w) + p["b_in"][:, off:off + width]

    def head_proj(hd):
        return (proj(OFF_Q + hd * DK, DK), proj(OFF_K + hd * DK, DK),
                proj(OFF_V + hd * DV, DV), proj(OFF_RZ + hd * DV, DV))

    def head_rope(raw):
        q, k, v, z = raw
        return _rope(q, cos, sin), _rope(k, cos, sin), v.astype(BF16), z

    def head_norm(hd, o, z):
        mu = jnp.mean(o, axis=-1, keepdims=True)
        oc = o - mu
        var = jnp.mean(oc * oc, axis=-1, keepdims=True)
        on = oc * lax.rsqrt(var + GN_EPS) * p["gn_g"][:, hd * DV:(hd + 1) * DV]
        return (on * _silu(z)).astype(BF16)

    def head_out(hd, on):
        return _dot(on, _weight(p["w_ro"], slice(hd * DV, (hd + 1) * DV), slice(None)))

    raw = head_proj(0)
    y_b = None
    pending = None
    for hd in range(N_HEADS):
        q, k, v, z = head_rope(raw)
        on = None if pending is None else head_norm(*pending)
        if hd + 1 < N_HEADS:
            raw = head_proj(hd + 1)
        else:
            c_c, c_h = proj(OFF_CC, D_CONV), proj(OFF_CH, D_CONV)
        if on is not None:
            y_hd = head_out(pending[0], on)
            y_b = y_hd if y_b is None else y_b + y_hd
        pending = (hd, retention(hd, q, k, v), z)

    on = head_norm(*pending)
    u = c_c * c_h
    u_ref[SUBLANES:SUBLANES + rows, :] = u
    conv = p["conv_b"][...] + p["conv_w"][2:3, :] * u
    conv = conv + p["conv_w"][1:2, :] * u_ref[SUBLANES - 1:SUBLANES - 1 + rows, :]
    conv = conv + p["conv_w"][0:1, :] * u_ref[SUBLANES - 2:SUBLANES - 2 + rows, :]
    c_b, c_z = proj(OFF_CB, D_CONV), proj(OFF_CZ, D_CONV)
    y_b = y_b + head_out(pending[0], on)
    y = (c_b * conv * _silu(c_z)).astype(BF16)
    g_a, g_b = proj(OFF_GA, D_MODEL), proj(OFF_GB, D_MODEL)
    y_a = _dot(y, _weight(p["w_co"], slice(0, D_CONV), slice(None)))
    mix = jax.nn.sigmoid(g_a) * y_a + jax.nn.sigmoid(g_b) * y_b
    y_o = _dot(mix.astype(BF16), _weight(p["w_o"], slice(0, D_MODEL), slice(None)))
    return h + _rmsnorm(y_o, p["post_g"][...])


def _cat(halves):
    return jnp.concatenate(halves, axis=1)


def _main_kernel(h_ref, cos_ref, sin_ref, dmat_ref, qdec_ref, kdec_ref, s_init_ref, utail_ref,
                 pre_g, b_in, conv_w, conv_b, gn_g, post_g, w_in, w_co, w_ro, w_o,
                 out_ref, state_ref, u_ref):
    rows = h_ref.shape[0]

    @pl.when(pl.program_id(1) == 0)
    def _():
        state_ref[...] = s_init_ref[...]
        u_ref[0:SUBLANES, :] = utail_ref[...]

    gammas = _gammas()

    def retention(hd, q, k, v):
        outs = []
        for c in range(rows // SUPER):
            r = slice(c * SUPER, (c + 1) * SUPER)
            q1, q2 = q[0][r], q[1][r]
            k1, k2 = k[0][r], k[1][r]
            vc = v[r]
            qdec = qdec_ref[hd]
            kdec = kdec_ref[hd]
            qr = _cat([q1, q2]).astype(BF16)
            kr = _cat([k1, k2]).astype(BF16)
            qd = _cat([q1 * qdec, q2 * qdec]).astype(BF16)
            kd = _cat([k1 * kdec, k2 * kdec]).astype(BF16)
            s = _dot_nt(qr, kr)
            state = state_ref[hd]
            cross = _dot(qd, state.astype(BF16))
            state_ref[hd] = (gammas[hd] ** SUPER) * state + _dot_tn(kd, vc)
            outs.append(_dot((s * dmat_ref[hd]).astype(BF16), vc) + cross)
        return outs[0] if len(outs) == 1 else jnp.concatenate(outs, axis=0)

    p = dict(pre_g=pre_g, b_in=b_in, conv_w=conv_w, conv_b=conv_b, gn_g=gn_g, post_g=post_g,
             w_in=w_in, w_co=w_co, w_ro=w_ro, w_o=w_o)
    out_ref[...] = _layer_rows(h_ref[...], cos_ref[...], sin_ref[...], p, u_ref, retention)
    u_ref[0:SUBLANES, :] = u_ref[rows:rows + SUBLANES, :]


def _meta_kernel(meta_ref, cos_ref, sin_ref, dmat_ref, kdec_ref,
                 pre_g, b_in, conv_w, conv_b, gn_g, post_g, w_in, w_co, w_ro, w_o,
                 s_out_ref, utail_ref, hm_ref, u_ref):
    @pl.when(pl.program_id(0) == 0)
    def _():
        hm_ref[...] = meta_ref[...]
        u_ref[0:SUBLANES, :] = jnp.zeros((SUBLANES, D_MODEL), F32)

    def retention(hd, q, k, v):
        kdec = kdec_ref[hd]
        qr = _cat(q).astype(BF16)
        kr = _cat(k).astype(BF16)
        kd = _cat([k[0] * kdec, k[1] * kdec]).astype(BF16)
        s = (_dot_nt(qr, kr) * dmat_ref[hd]).astype(BF16)
        s_out_ref[hd] = _dot_tn(kd, v)
        return _dot(s, v)

    p = dict(pre_g=pre_g, b_in=b_in, conv_w=conv_w, conv_b=conv_b, gn_g=gn_g, post_g=post_g,
             w_in=w_in, w_co=w_co, w_ro=w_ro, w_o=w_o)
    hm_ref[...] = _layer_rows(hm_ref[...], cos_ref[...], sin_ref[...], p, u_ref, retention)
    utail_ref[...] = u_ref[N_META:N_META + SUBLANES, :]


def _tables(seq):
    total = N_META + seq
    pos = jnp.arange(total, dtype=F32)
    inv = ROPE_BASE ** (-jnp.arange(HALF, dtype=F32) / HALF)
    ang = pos[:, None] * inv[None, :]
    cos, sin = jnp.cos(ang), jnp.sin(ang)

    log_g = jnp.log(1.0 - 2.0 ** (-5.0 - jnp.arange(N_HEADS, dtype=F32)))[:, None, None]
    scale = DK ** -0.5
    i = jnp.arange(SUPER, dtype=F32)
    dist = jnp.abs(i[:, None] - i[None, :])
    visible = (jnp.arange(SUPER)[None, :] // CHUNK) <= (jnp.arange(SUPER)[:, None] // CHUNK)
    dmat = jnp.where(visible[None], jnp.exp(log_g * dist[None]), 0.0) * scale
    lane = jnp.ones((1, 1, HALF), F32)
    qdec = jnp.exp(log_g * (i[None, :, None] + 1.0)) * lane
    kdec = jnp.exp(log_g * (SUPER - 1.0 - i[None, :, None])) * scale * lane

    m = jnp.arange(META_ROWS, dtype=F32)
    is_meta = jnp.arange(META_ROWS) < N_META
    mdist = jnp.abs(m[:, None] - m[None, :])
    dmat_m = jnp.where((is_meta[:, None] & is_meta[None, :])[None], jnp.exp(log_g * mdist[None]), 0.0) * scale
    kdec_m = jnp.where(is_meta[None, :, None], jnp.exp(log_g * (N_META - 1.0 - m[None, :, None])), 0.0)
    kdec_m = kdec_m * scale * lane
    return dict(cos=cos, sin=sin, dmat=dmat, qdec=qdec, kdec=kdec, dmat_m=dmat_m, kdec_m=kdec_m)


def _resident(tail_shape, index_map):
    return pl.BlockSpec((None,) + tail_shape, index_map, pipeline_mode=pl.Buffered(1))


def _param_specs(params, layer_index_map):
    return [_resident(a.shape[1:], functools.partial(layer_index_map, nd=a.ndim - 1)) for a in params]


def _meta_layer_map(l, nd):
    return (l,) + (0,) * nd


def _main_layer_map(b, t, layer, nd):
    return (layer,) + (0,) * nd


def _const_map2(*_):
    return (0, 0)


def _const_map3(*_):
    return (0, 0, 0)


def _meta_call(meta_pad, tabs, params):
    in_specs = [
        pl.BlockSpec((META_ROWS, D_MODEL), _const_map2),
        pl.BlockSpec((META_ROWS, HALF), _const_map2),
        pl.BlockSpec((META_ROWS, HALF), _const_map2),
        pl.BlockSpec((N_HEADS, META_ROWS, META_ROWS), _const_map3),
        pl.BlockSpec((N_HEADS, META_ROWS, HALF), _const_map3),
    ] + _param_specs(params, _meta_layer_map)
    out_shape = (jax.ShapeDtypeStruct((DEPTH, N_HEADS, DK, DV), F32),
                 jax.ShapeDtypeStruct((DEPTH, SUBLANES, D_MODEL), F32))
    out_specs = (pl.BlockSpec((None, N_HEADS, DK, DV), lambda l: (l, 0, 0, 0)),
                 pl.BlockSpec((None, SUBLANES, D_MODEL), lambda l: (l, 0, 0)))
    return pl.pallas_call(
        _meta_kernel,
        grid=(DEPTH,),
        in_specs=in_specs,
        out_specs=out_specs,
        out_shape=out_shape,
        scratch_shapes=[pltpu.VMEM((META_ROWS, D_MODEL), F32),
                        pltpu.VMEM((SUBLANES + META_ROWS, D_MODEL), F32)],
        compiler_params=pltpu.CompilerParams(dimension_semantics=("arbitrary",),
                                             vmem_limit_bytes=VMEM_LIMIT_BYTES),
        name="meta_tokens",
    )(meta_pad, tabs["cos"][:META_ROWS], tabs["sin"][:META_ROWS], tabs["dmat_m"], tabs["kdec_m"], *params)


def _main_call(h, layer, tabs, s_init, utail, params):
    batch, seq, _ = h.shape
    layer_map = functools.partial(_main_layer_map, layer=layer)
    in_specs = [
        pl.BlockSpec((None, TOKEN_TILE, D_MODEL), lambda b, t: (b, t, 0)),
        pl.BlockSpec((TOKEN_TILE, HALF), lambda b, t: (t, 0)),
        pl.BlockSpec((TOKEN_TILE, HALF), lambda b, t: (t, 0)),
        pl.BlockSpec((N_HEADS, SUPER, SUPER), _const_map3, pipeline_mode=pl.Buffered(1)),
        pl.BlockSpec((N_HEADS, SUPER, HALF), _const_map3, pipeline_mode=pl.Buffered(1)),
        pl.BlockSpec((N_HEADS, SUPER, HALF), _const_map3, pipeline_mode=pl.Buffered(1)),
        _resident(s_init.shape[1:], functools.partial(layer_map, nd=3)),
        _resident(utail.shape[1:], functools.partial(layer_map, nd=2)),
    ] + _param_specs(params, layer_map)
    return pl.pallas_call(
        _main_kernel,
        grid=(batch, seq // TOKEN_TILE),
        in_specs=in_specs,
        out_specs=pl.BlockSpec((None, TOKEN_TILE, D_MODEL), lambda b, t: (b, t, 0)),
        out_shape=jax.ShapeDtypeStruct(h.shape, F32),
        scratch_shapes=[pltpu.VMEM((N_HEADS, DK, DV), F32),
                        pltpu.VMEM((SUBLANES + TOKEN_TILE, D_MODEL), F32)],
        compiler_params=pltpu.CompilerParams(dimension_semantics=("arbitrary", "arbitrary"),
                                             vmem_limit_bytes=VMEM_LIMIT_BYTES),
        name="mixer_layer",
    )(h, tabs["cos"][N_META:], tabs["sin"][N_META:], tabs["dmat"], tabs["qdec"], tabs["kdec"],
      s_init, utail, *params)


def kernel(x, meta, pre_norm_g, w_in, b_in, conv_w, conv_b, w_conv_out, ret_gn_g, w_ret_out, w_o, post_norm_g):
    batch, seq, d_model = x.shape
    assert d_model == D_MODEL and seq % TOKEN_TILE == 0 and meta.shape == (N_META, D_MODEL)
    assert w_in.shape == (DEPTH, D_MODEL, D_IN)
    tabs = _tables(seq)
    params = (
        pre_norm_g.reshape(DEPTH, 1, D_MODEL),
        b_in.reshape(DEPTH, 1, D_IN),
        conv_w,
        conv_b.reshape(DEPTH, 1, D_CONV),
        ret_gn_g.reshape(DEPTH, 1, D_V),
        post_norm_g.reshape(DEPTH, 1, D_MODEL),
        _pack_rows(w_in),
        _pack_rows(w_conv_out),
        _pack_rows(w_ret_out),
        _pack_rows(w_o),
    )
    meta_pad = jnp.zeros((META_ROWS, D_MODEL), F32).at[:N_META].set(meta.astype(F32))
    s_init, utail = _meta_call(meta_pad, tabs, params)
    h = x
    for layer in range(DEPTH):
        h = _main_call(h, layer, tabs, s_init, utail, params)
    return h
```

```python
import functools

import jax
import jax.numpy as jnp
from jax import lax
from jax.experimental import pallas as pl
from jax.experimental.pallas import tpu as pltpu

D_MODEL = 1024
DEPTH = 4
CHUNK = 64
N_META = 16
CONV_WIDTH = 3
D_CONV = D_MODEL
N_HEADS = 4
DK = 256
DV = 512
D_QK = N_HEADS * DK
D_V = N_HEADS * DV
ROPE_BASE = 10000.0
RMS_EPS = 1e-6
GN_EPS = 1e-5
D_IN = 4 * D_CONV + 2 * D_QK + 2 * D_V + 2 * D_MODEL

OFF_CH = 0
OFF_CB = OFF_CH + D_CONV
OFF_CC = OFF_CB + D_CONV
OFF_CZ = OFF_CC + D_CONV
OFF_Q = OFF_CZ + D_CONV
OFF_K = OFF_Q + D_QK
OFF_V = OFF_K + D_QK
OFF_RZ = OFF_V + D_V
OFF_GA = OFF_RZ + D_V
OFF_GB = OFF_GA + D_MODEL

HALF = DK // 2
SUPER = 4 * CHUNK
TOKEN_TILE = SUPER
META_ROWS = 128
SUBLANES = 8
VMEM_LIMIT_BYTES = 60 * 1024 * 1024

F32 = jnp.float32
BF16 = jnp.bfloat16


def _gammas():
    return [1.0 - 2.0 ** (-5.0 - h) for h in range(N_HEADS)]


ROW_PACK = 2


PACK_BLOCK_ROWS = 512
PACK_BLOCK_COLS = 2048


def _pack_kernel(w_ref, o_ref):
    o_ref[...] = pltpu.bitcast(w_ref[...].astype(BF16), jnp.uint32)


def _pack_rows(w):
    layers, k, n = w.shape
    bk, bn = min(k, PACK_BLOCK_ROWS), min(n, PACK_BLOCK_COLS)
    assert k % bk == 0 and n % bn == 0
    return pl.pallas_call(
        _pack_kernel,
        grid=(layers, k // bk, n // bn),
        in_specs=[pl.BlockSpec((None, bk, bn), lambda l, i, j: (l, i, j))],
        out_specs=pl.BlockSpec((None, bk // ROW_PACK, bn), lambda l, i, j: (l, i, j)),
        out_shape=jax.ShapeDtypeStruct((layers, k // ROW_PACK, n), jnp.uint32),
        compiler_params=pltpu.CompilerParams(dimension_semantics=("arbitrary", "arbitrary", "arbitrary")),
        name="pack_weight_rows",
    )(w)


def _weight(w_ref, rows, cols):
    packed = w_ref[rows.start // ROW_PACK:rows.stop // ROW_PACK, cols]
    return pltpu.bitcast(packed, BF16)


def _dot(a, b):
    return jnp.dot(a, b, preferred_element_type=F32)


def _dot_nt(a, b):
    return lax.dot_general(a, b, (((1,), (1,)), ((), ())), preferred_element_type=F32)


def _dot_tn(a, b):
    return lax.dot_general(a, b, (((0,), (0,)), ((), ())), preferred_element_type=F32)


def _rmsnorm(x, g):
    ms = jnp.mean(x * x, axis=-1, keepdims=True)
    return x * lax.rsqrt(ms + RMS_EPS) * g


def _silu(x):
    return x * jax.nn.sigmoid(x)


def _rope(x, cos, sin):
    x1, x2 = x[:, :HALF], x[:, HALF:]
    return x1 * cos - x2 * sin, x1 * sin + x2 * cos


def _layer_rows(h, cos, sin, p, u_ref, retention):
    rows = h.shape[0]
    xn = _rmsnorm(h, p["pre_g"][...]).astype(BF16)

    def proj(off, width):
        w = _weight(p["w_in"], slice(0, D_MODEL), slice(off, off + width))
        return _dot(xn, w) + p["b_in"][:, off:off + width]

    def head_proj(hd):
        return (proj(OFF_Q + hd * DK, DK), proj(OFF_K + hd * DK, DK),
                proj(OFF_V + hd * DV, DV), proj(OFF_RZ + hd * DV, DV))

    def head_rope(raw):
        q, k, v, z = raw
        return _rope(q, cos, sin), _rope(k, cos, sin), v.astype(BF16), z

    def head_norm(hd, o, z):
        mu = jnp.mean(o, axis=-1, keepdims=True)
        oc = o - mu
        var = jnp.mean(oc * oc, axis=-1, keepdims=True)
        on = oc * lax.rsqrt(var + GN_EPS) * p["gn_g"][:, hd * DV:(hd + 1) * DV]
        return (on * _silu(z)).astype(BF16)

    def head_out(hd, on):
        return _dot(on, _weight(p["w_ro"], slice(hd * DV, (hd + 1) * DV), slice(None)))

    raw = head_proj(0)
    y_b = None
    pending = None
    for hd in range(N_HEADS):
        q, k, v, z = head_rope(raw)
        on = None if pending is None else head_norm(*pending)
        if hd + 1 < N_HEADS:
            raw = head_proj(hd + 1)
        else:
            c_c, c_h = proj(OFF_CC, D_CONV), proj(OFF_CH, D_CONV)
        if on is not None:
            y_hd = head_out(pending[0], on)
            y_b = y_hd if y_b is None else y_b + y_hd
        pending = (hd, retention(hd, q, k, v), z)

    on = head_norm(*pending)
    u = c_c * c_h
    u_ref[SUBLANES:SUBLANES + rows, :] = u
    conv = p["conv_b"][...] + p["conv_w"][2:3, :] * u
    conv = conv + p["conv_w"][1:2, :] * u_ref[SUBLANES - 1:SUBLANES - 1 + rows, :]
    conv = conv + p["conv_w"][0:1, :] * u_ref[SUBLANES - 2:SUBLANES - 2 + rows, :]
    c_b, c_z = proj(OFF_CB, D_CONV), proj(OFF_CZ, D_CONV)
    y_b = y_b + head_out(pending[0], on)
    y = (c_b * conv * _silu(c_z)).astype(BF16)
    g_a, g_b = proj(OFF_GA, D_MODEL), proj(OFF_GB, D_MODEL)
    y_a = _dot(y, _weight(p["w_co"], slice(0, D_CONV), slice(None)))
    mix = jax.nn.sigmoid(g_a) * y_a + jax.nn.sigmoid(g_b) * y_b
    y_o = _dot(mix.astype(BF16), _weight(p["w_o"], slice(0, D_MODEL), slice(None)))
    return h + _rmsnorm(y_o, p["post_g"][...])


def _cat(halves):
    return jnp.concatenate(halves, axis=1)


def _main_kernel(h_ref, cos_ref, sin_ref, dmat_ref, qdec_ref, kdec_ref, s_init_ref, utail_ref,
                 pre_g, b_in, conv_w, conv_b, gn_g, post_g, w_in, w_co, w_ro, w_o,
                 out_ref, state_ref, u_ref):
    rows = h_ref.shape[0]

    @pl.when(pl.program_id(1) == 0)
    def _():
        state_ref[...] = s_init_ref[...]
        u_ref[0:SUBLANES, :] = utail_ref[...]

    gammas = _gammas()

    def retention(hd, q, k, v):
        outs = []
        for c in range(rows // SUPER):
            r = slice(c * SUPER, (c + 1) * SUPER)
            q1, q2 = q[0][r], q[1][r]
            k1, k2 = k[0][r], k[1][r]
            vc = v[r]
            qdec = qdec_ref[hd]
            kdec = kdec_ref[hd]
            qr = _cat([q1, q2]).astype(BF16)
            kr = _cat([k1, k2]).astype(BF16)
            qd = _cat([q1 * qdec, q2 * qdec]).astype(BF16)
            kd = _cat([k1 * kdec, k2 * kdec]).astype(BF16)
            s = _dot_nt(qr, kr)
            state = state_ref[hd]
            cross = _dot(qd, state.astype(BF16))
            state_ref[hd] = (gammas[hd] ** SUPER) * state + _dot_tn(kd, vc)
            outs.append(_dot((s * dmat_ref[hd]).astype(BF16), vc) + cross)
        return outs[0] if len(outs) == 1 else jnp.concatenate(outs, axis=0)

    p = dict(pre_g=pre_g, b_in=b_in, conv_w=conv_w, conv_b=conv_b, gn_g=gn_g, post_g=post_g,
             w_in=w_in, w_co=w_co, w_ro=w_ro, w_o=w_o)
    out_ref[...] = _layer_rows(h_ref[...], cos_ref[...], sin_ref[...], p, u_ref, retention)
    u_ref[0:SUBLANES, :] = u_ref[rows:rows + SUBLANES, :]


def _meta_kernel(meta_ref, cos_ref, sin_ref, dmat_ref, kdec_ref,
                 pre_g, b_in, conv_w, conv_b, gn_g, post_g, w_in, w_co, w_ro, w_o,
                 s_out_ref, utail_ref, hm_ref, u_ref):
    @pl.when(pl.program_id(0) == 0)
    def _():
        hm_ref[...] = meta_ref[...]
        u_ref[0:SUBLANES, :] = jnp.zeros((SUBLANES, D_MODEL), F32)

    def retention(hd, q, k, v):
        kdec = kdec_ref[hd]
        qr = _cat(q).astype(BF16)
        kr = _cat(k).astype(BF16)
        kd = _cat([k[0] * kdec, k[1] * kdec]).astype(BF16)
        s = (_dot_nt(qr, kr) * dmat_ref[hd]).astype(BF16)
        s_out_ref[hd] = _dot_tn(kd, v)
        return _dot(s, v)

    p = dict(pre_g=pre_g, b_in=b_in, conv_w=conv_w, conv_b=conv_b, gn_g=gn_g, post_g=post_g,
             w_in=w_in, w_co=w_co, w_ro=w_ro, w_o=w_o)
    hm_ref[...] = _layer_rows(hm_ref[...], cos_ref[...], sin_ref[...], p, u_ref, retention)
    utail_ref[...] = u_ref[N_META:N_META + SUBLANES, :]


def _tables(seq):
    total = N_META + seq
    pos = jnp.arange(total, dtype=F32)
    inv = ROPE_BASE ** (-jnp.arange(HALF, dtype=F32) / HALF)
    ang = pos[:, None] * inv[None, :]
    cos, sin = jnp.cos(ang), jnp.sin(ang)

    log_g = jnp.log(1.0 - 2.0 ** (-5.0 - jnp.arange(N_HEADS, dtype=F32)))[:, None, None]
    scale = DK ** -0.5
    i = jnp.arange(SUPER, dtype=F32)
    dist = jnp.abs(i[:, None] - i[None, :])
    visible = (jnp.arange(SUPER)[None, :] // CHUNK) <= (jnp.arange(SUPER)[:, None] // CHUNK)
    dmat = jnp.where(visible[None], jnp.exp(log_g * dist[None]), 0.0) * scale
    lane = jnp.ones((1, 1, HALF), F32)
    qdec = jnp.exp(log_g * (i[None, :, None] + 1.0)) * lane
    kdec = jnp.exp(log_g * (SUPER - 1.0 - i[None, :, None])) * scale * lane

    m = jnp.arange(META_ROWS, dtype=F32)
    is_meta = jnp.arange(META_ROWS) < N_META
    mdist = jnp.abs(m[:, None] - m[None, :])
    dmat_m = jnp.where((is_meta[:, None] & is_meta[None, :])[None], jnp.exp(log_g * mdist[None]), 0.0) * scale
    kdec_m = jnp.where(is_meta[None, :, None], jnp.exp(log_g * (N_META - 1.0 - m[None, :, None])), 0.0)
    kdec_m = kdec_m * scale * lane
    return dict(cos=cos, sin=sin, dmat=dmat, qdec=qdec, kdec=kdec, dmat_m=dmat_m, kdec_m=kdec_m)


def _resident(tail_shape, index_map):
    return pl.BlockSpec((None,) + tail_shape, index_map, pipeline_mode=pl.Buffered(1))


def _param_specs(params, layer_index_map):
    return [_resident(a.shape[1:], functools.partial(layer_index_map, nd=a.ndim - 1)) for a in params]


def _meta_layer_map(l, nd):
    return (l,) + (0,) * nd


def _main_layer_map(b, t, layer, nd):
    return (layer,) + (0,) * nd


def _const_map2(*_):
    return (0, 0)


def _const_map3(*_):
    return (0, 0, 0)


def _meta_call(meta_pad, tabs, params):
    in_specs = [
        pl.BlockSpec((META_ROWS, D_MODEL), _const_map2),
        pl.BlockSpec((META_ROWS, HALF), _const_map2),
        pl.BlockSpec((META_ROWS, HALF), _const_map2),
        pl.BlockSpec((N_HEADS, META_ROWS, META_ROWS), _const_map3),
        pl.BlockSpec((N_HEADS, META_ROWS, HALF), _const_map3),
    ] + _param_specs(params, _meta_layer_map)
    out_shape = (jax.ShapeDtypeStruct((DEPTH, N_HEADS, DK, DV), F32),
                 jax.ShapeDtypeStruct((DEPTH, SUBLANES, D_MODEL), F32))
    out_specs = (pl.BlockSpec((None, N_HEADS, DK, DV), lambda l: (l, 0, 0, 0)),
                 pl.BlockSpec((None, SUBLANES, D_MODEL), lambda l: (l, 0, 0)))
    return pl.pallas_call(
        _meta_kernel,
        grid=(DEPTH,),
        in_specs=in_specs,
        out_specs=out_specs,
        out_shape=out_shape,
        scratch_shapes=[pltpu.VMEM((META_ROWS, D_MODEL), F32),
                        pltpu.VMEM((SUBLANES + META_ROWS, D_MODEL), F32)],
        compiler_params=pltpu.CompilerParams(dimension_semantics=("arbitrary",),
                                             vmem_limit_bytes=VMEM_LIMIT_BYTES),
        name="meta_tokens",
    )(meta_pad, tabs["cos"][:META_ROWS], tabs["sin"][:META_ROWS], tabs["dmat_m"], tabs["kdec_m"], *params)


def _main_call(h, layer, tabs, s_init, utail, params):
    batch, seq, _ = h.shape
    layer_map = functools.partial(_main_layer_map, layer=layer)
    in_specs = [
        pl.BlockSpec((None, TOKEN_TILE, D_MODEL), lambda b, t: (b, t, 0)),
        pl.BlockSpec((TOKEN_TILE, HALF), lambda b, t: (t, 0)),
        pl.BlockSpec((TOKEN_TILE, HALF), lambda b, t: (t, 0)),
        pl.BlockSpec((N_HEADS, SUPER, SUPER), _const_map3, pipeline_mode=pl.Buffered(1)),
        pl.BlockSpec((N_HEADS, SUPER, HALF), _const_map3, pipeline_mode=pl.Buffered(1)),
        pl.BlockSpec((N_HEADS, SUPER, HALF), _const_map3, pipeline_mode=pl.Buffered(1)),
        _resident(s_init.shape[1:], functools.partial(layer_map, nd=3)),
        _resident(utail.shape[1:], functools.partial(layer_map, nd=2)),
    ] + _param_specs(params, layer_map)
    return pl.pallas_call(
        _main_kernel,
        grid=(batch, seq // TOKEN_TILE),
        in_specs=in_specs,
        out_specs=pl.BlockSpec((None, TOKEN_TILE, D_MODEL), lambda b, t: (b, t, 0)),
        out_shape=jax.ShapeDtypeStruct(h.shape, F32),
        scratch_shapes=[pltpu.VMEM((N_HEADS, DK, DV), F32),
                        pltpu.VMEM((SUBLANES + TOKEN_TILE, D_MODEL), F32)],
        compiler_params=pltpu.CompilerParams(dimension_semantics=("arbitrary", "arbitrary"),
                                             vmem_limit_bytes=VMEM_LIMIT_BYTES),
        name="mixer_layer",
    )(h, tabs["cos"][N_META:], tabs["sin"][N_META:], tabs["dmat"], tabs["qdec"], tabs["kdec"],
      s_init, utail, *params)


def kernel(x, meta, pre_norm_g, w_in, b_in, conv_w, conv_b, w_conv_out, ret_gn_g, w_ret_out, w_o, post_norm_g):
    batch, seq, d_model = x.shape
    assert d_model == D_MODEL and seq % TOKEN_TILE == 0 and meta.shape == (N_META, D_MODEL)
    assert w_in.shape == (DEPTH, D_MODEL, D_IN)
    tabs = _tables(seq)
    params = (
        pre_norm_g.reshape(DEPTH, 1, D_MODEL),
        b_in.reshape(DEPTH, 1, D_IN),
        conv_w,
        conv_b.reshape(DEPTH, 1, D_CONV),
        ret_gn_g.reshape(DEPTH, 1, D_V),
        post_norm_g.reshape(DEPTH, 1, D_MODEL),
        _pack_rows(w_in),
        _pack_rows(w_conv_out),
        _pack_rows(w_ret_out),
        _pack_rows(w_o),
    )
    meta_pad = jnp.zeros((META_ROWS, D_MODEL), F32).at[:N_META].set(meta.astype(F32))
    s_init, utail = _meta_call(meta_pad, tabs, params)
    h = x
    for layer in range(DEPTH):
        h = _main_call(h, layer, tabs, s_init, utail, params)
    return h
```

```python
import functools

import jax
import jax.numpy as jnp
from jax import lax
from jax.experimental import pallas as pl
from jax.experimental.pallas import tpu as pltpu

D_MODEL = 1024
DEPTH = 4
CHUNK = 64
N_META = 16
CONV_WIDTH = 3
D_CONV = D_MODEL
N_HEADS = 4
DK = 256
DV = 512
D_QK = N_HEADS * DK
D_V = N_HEADS * DV
ROPE_BASE = 10000.0
RMS_EPS = 1e-6
GN_EPS = 1e-5
D_IN = 4 * D_CONV + 2 * D_QK + 2 * D_V + 2 * D_MODEL

OFF_CH = 0
OFF_CB = OFF_CH + D_CONV
OFF_CC = OFF_CB + D_CONV
OFF_CZ = OFF_CC + D_CONV
OFF_Q = OFF_CZ + D_CONV
OFF_K = OFF_Q + D_QK
OFF_V = OFF_K + D_QK
OFF_RZ = OFF_V + D_V
OFF_GA = OFF_RZ + D_V
OFF_GB = OFF_GA + D_MODEL

HALF = DK // 2
SUPER = 4 * CHUNK
TOKEN_TILE = SUPER
META_ROWS = 128
SUBLANES = 8
VMEM_LIMIT_BYTES = 60 * 1024 * 1024
ROW_PACK = 2
PACK_BLOCK_ROWS = 512
PACK_BLOCK_COLS = 2048
PIPELINE_FILL_STEPS = 2
NORM_PARTS = 4

F32 = jnp.float32
BF16 = jnp.bfloat16


def _gammas():
    return [1.0 - 2.0 ** (-5.0 - h) for h in range(N_HEADS)]


def _pack_kernel(w_ref, o_ref):
    o_ref[...] = pltpu.bitcast(w_ref[...].astype(BF16), jnp.uint32)


def _pack_rows(w):
    layers, k, n = w.shape
    bk, bn = min(k, PACK_BLOCK_ROWS), min(n, PACK_BLOCK_COLS)
    assert k % bk == 0 and n % bn == 0
    return pl.pallas_call(
        _pack_kernel,
        grid=(layers, k // bk, n // bn),
        in_specs=[pl.BlockSpec((None, bk, bn), lambda l, i, j: (l, i, j))],
        out_specs=pl.BlockSpec((None, bk // ROW_PACK, bn), lambda l, i, j: (l, i, j)),
        out_shape=jax.ShapeDtypeStruct((layers, k // ROW_PACK, n), jnp.uint32),
        compiler_params=pltpu.CompilerParams(dimension_semantics=("arbitrary", "arbitrary", "arbitrary")),
        name="pack_weight_rows",
    )(w)


def _weight(w_ref, rows, cols):
    packed = w_ref[rows.start // ROW_PACK:rows.stop // ROW_PACK, cols]
    return pltpu.bitcast(packed, BF16)


def _dot(a, b):
    return jnp.dot(a, b, preferred_element_type=F32)


def _dot_nt(a, b):
    return lax.dot_general(a, b, (((1,), (1,)), ((), ())), preferred_element_type=F32)


def _dot_tn(a, b):
    return lax.dot_general(a, b, (((0,), (0,)), ((), ())), preferred_element_type=F32)


def _rmsnorm(x, g):
    ms = jnp.mean(x * x, axis=-1, keepdims=True)
    return x * lax.rsqrt(ms + RMS_EPS) * g


def _silu(x):
    return x * jax.nn.sigmoid(x)


def _rope(x, cos, sin):
    x1, x2 = x[:, :HALF], x[:, HALF:]
    return x1 * cos - x2 * sin, x1 * sin + x2 * cos


def _pre_norm(h, p):
    return _rmsnorm(h, p["pre_g"][...]).astype(BF16)


def _post_norm(h, y_o, p):
    return h + _rmsnorm(y_o, p["post_g"][...])


def _zero_row(x):
    bits = lax.bitcast_convert_type(x, jnp.int32)
    half_word = jnp.asarray(16, jnp.int32)
    return lax.shift_right_logical(lax.shift_right_logical(bits, half_word), half_word).astype(F32)


def _mixer_rows(xn, cos, sin, p, u_ref, retention, side_work=None):
    rows = xn.shape[0]

    def proj(off, width):
        w = _weight(p["w_in"], slice(0, D_MODEL), slice(off, off + width))
        return _dot(xn, w) + p["b_in"][:, off:off + width]

    def head_proj(hd):
        return (proj(OFF_Q + hd * DK, DK), proj(OFF_K + hd * DK, DK),
                proj(OFF_V + hd * DV, DV), proj(OFF_RZ + hd * DV, DV))

    def head_rope(raw):
        q, k, v, z = raw
        return _rope(q, cos, sin), _rope(k, cos, sin), v.astype(BF16), z

    def head_norm(hd, o, z):
        mu = jnp.mean(o, axis=-1, keepdims=True)
        oc = o - mu
        var = jnp.mean(oc * oc, axis=-1, keepdims=True)
        on = oc * lax.rsqrt(var + GN_EPS) * p["gn_g"][:, hd * DV:(hd + 1) * DV]
        return (on * _silu(z)).astype(BF16)

    def head_out(hd, on):
        return _dot(on, _weight(p["w_ro"], slice(hd * DV, (hd + 1) * DV), slice(None)))

    if side_work:
        side_rows = [work() for work in side_work]
        cos = cos + _zero_row(sum(side_rows[1:], side_rows[0]))[:, :HALF]
    raw = head_proj(0)
    y_b = None
    pending = None
    for hd in range(N_HEADS):
        q, k, v, z = head_rope(raw)
        on = None if pending is None else head_norm(*pending)
        if hd + 1 < N_HEADS:
            raw = head_proj(hd + 1)
        else:
            c_c, c_h = proj(OFF_CC, D_CONV), proj(OFF_CH, D_CONV)
        if on is not None:
            y_hd = head_out(pending[0], on)
            y_b = y_hd if y_b is None else y_b + y_hd
        pending = (hd, retention(hd, q, k, v), z)

    on = head_norm(*pending)
    u = c_c * c_h
    u_ref[SUBLANES:SUBLANES + rows, :] = u
    conv = p["conv_b"][...] + p["conv_w"][2:3, :] * u
    conv = conv + p["conv_w"][1:2, :] * u_ref[SUBLANES - 1:SUBLANES - 1 + rows, :]
    conv = conv + p["conv_w"][0:1, :] * u_ref[SUBLANES - 2:SUBLANES - 2 + rows, :]
    c_b, c_z = proj(OFF_CB, D_CONV), proj(OFF_CZ, D_CONV)
    y_b = y_b + head_out(pending[0], on)
    y = (c_b * conv * _silu(c_z)).astype(BF16)
    g_a, g_b = proj(OFF_GA, D_MODEL), proj(OFF_GB, D_MODEL)
    y_a = _dot(y, _weight(p["w_co"], slice(0, D_CONV), slice(None)))
    mix = jax.nn.sigmoid(g_a) * y_a + jax.nn.sigmoid(g_b) * y_b
    return _dot(mix.astype(BF16), _weight(p["w_o"], slice(0, D_MODEL), slice(None)))


def _cat(halves):
    return jnp.concatenate(halves, axis=1)


def _rem(x, n):
    return lax.rem(x, jnp.asarray(n, x.dtype))


def _main_kernel(hf_ref, hb_ref, cos_ref, sin_ref, dmat_ref, qdec_ref, kdec_ref, s_init_ref, utail_ref,
                 pre_g, b_in, conv_w, conv_b, gn_g, post_g, w_in, w_co, w_ro, w_o,
                 out_ref, state_ref, u_ref, xn_ref, yo_ref, *, tiles_per_row):
    step = pl.program_id(0)
    rows = hf_ref.shape[0]

    @pl.when(step == 0)
    def _():
        xn_ref[...] = jnp.zeros(xn_ref.shape, xn_ref.dtype)
        yo_ref[...] = jnp.zeros(yo_ref.shape, yo_ref.dtype)

    @pl.when((step == 0) | (_rem(step + tiles_per_row - 1, tiles_per_row) == 0))
    def _():
        state_ref[...] = s_init_ref[...]
        u_ref[0:SUBLANES, :] = utail_ref[...]

    gammas = _gammas()

    def retention(hd, q, k, v):
        outs = []
        for c in range(rows // SUPER):
            r = slice(c * SUPER, (c + 1) * SUPER)
            q1, q2 = q[0][r], q[1][r]
            k1, k2 = k[0][r], k[1][r]
            vc = v[r]
            qdec = qdec_ref[hd]
            kdec = kdec_ref[hd]
            qr = _cat([q1, q2]).astype(BF16)
            kr = _cat([k1, k2]).astype(BF16)
            qd = _cat([q1 * qdec, q2 * qdec]).astype(BF16)
            kd = _cat([k1 * kdec, k2 * kdec]).astype(BF16)
            s = _dot_nt(qr, kr)
            state = state_ref[hd]
            cross = _dot(qd, state.astype(BF16))
            state_ref[hd] = (gammas[hd] ** SUPER) * state + _dot_tn(kd, vc)
            outs.append(_dot((s * dmat_ref[hd]).astype(BF16), vc) + cross)
        return outs[0] if len(outs) == 1 else jnp.concatenate(outs, axis=0)

    p = dict(pre_g=pre_g, b_in=b_in, conv_w=conv_w, conv_b=conv_b, gn_g=gn_g, post_g=post_g,
             w_in=w_in, w_co=w_co, w_ro=w_ro, w_o=w_o)
    part_rows = rows // NORM_PARTS
    xn_next = []

    def norms(part):
        r = pl.ds(part * part_rows, part_rows)
        out = _post_norm(hb_ref[r, :], yo_ref[r, :], p)
        out_ref[r, :] = out
        xf = _rmsnorm(hf_ref[r, :], pre_g[...])
        xn_next.append(xf.astype(BF16))
        return jnp.sum(out, axis=0, keepdims=True) + jnp.sum(xf, axis=0, keepdims=True)

    side = [functools.partial(norms, part) for part in range(NORM_PARTS)]
    y_o = _mixer_rows(xn_ref[...], cos_ref[...], sin_ref[...], p, u_ref, retention, side_work=side)
    yo_ref[...] = y_o
    for part, xn_part in enumerate(xn_next):
        xn_ref[pl.ds(part * part_rows, part_rows), :] = xn_part
    u_ref[0:SUBLANES, :] = u_ref[rows:rows + SUBLANES, :]


def _meta_kernel(meta_ref, cos_ref, sin_ref, dmat_ref, kdec_ref,
                 pre_g, b_in, conv_w, conv_b, gn_g, post_g, w_in, w_co, w_ro, w_o,
                 s_out_ref, utail_ref, hm_ref, u_ref):
    @pl.when(pl.program_id(0) == 0)
    def _():
        hm_ref[...] = meta_ref[...]
        u_ref[0:SUBLANES, :] = jnp.zeros((SUBLANES, D_MODEL), F32)

    def retention(hd, q, k, v):
        kdec = kdec_ref[hd]
        qr = _cat(q).astype(BF16)
        kr = _cat(k).astype(BF16)
        kd = _cat([k[0] * kdec, k[1] * kdec]).astype(BF16)
        s = (_dot_nt(qr, kr) * dmat_ref[hd]).astype(BF16)
        s_out_ref[hd] = _dot_tn(kd, v)
        return _dot(s, v)

    p = dict(pre_g=pre_g, b_in=b_in, conv_w=conv_w, conv_b=conv_b, gn_g=gn_g, post_g=post_g,
             w_in=w_in, w_co=w_co, w_ro=w_ro, w_o=w_o)
    hm = hm_ref[...]
    y_o = _mixer_rows(_pre_norm(hm, p), cos_ref[...], sin_ref[...], p, u_ref, retention)
    hm_ref[...] = _post_norm(hm, y_o, p)
    utail_ref[...] = u_ref[N_META:N_META + SUBLANES, :]


def _tables(seq):
    total = N_META + seq
    pos = jnp.arange(total, dtype=F32)
    inv = ROPE_BASE ** (-jnp.arange(HALF, dtype=F32) / HALF)
    ang = pos[:, None] * inv[None, :]
    cos, sin = jnp.cos(ang), jnp.sin(ang)

    log_g = jnp.log(1.0 - 2.0 ** (-5.0 - jnp.arange(N_HEADS, dtype=F32)))[:, None, None]
    scale = DK ** -0.5
    i = jnp.arange(SUPER, dtype=F32)
    dist = jnp.abs(i[:, None] - i[None, :])
    visible = (jnp.arange(SUPER)[None, :] // CHUNK) <= (jnp.arange(SUPER)[:, None] // CHUNK)
    dmat = jnp.where(visible[None], jnp.exp(log_g * dist[None]), 0.0) * scale
    lane = jnp.ones((1, 1, HALF), F32)
    qdec = jnp.exp(log_g * (i[None, :, None] + 1.0)) * lane
    kdec = jnp.exp(log_g * (SUPER - 1.0 - i[None, :, None])) * scale * lane

    m = jnp.arange(META_ROWS, dtype=F32)
    is_meta = jnp.arange(META_ROWS) < N_META
    mdist = jnp.abs(m[:, None] - m[None, :])
    dmat_m = jnp.where((is_meta[:, None] & is_meta[None, :])[None], jnp.exp(log_g * mdist[None]), 0.0) * scale
    kdec_m = jnp.where(is_meta[None, :, None], jnp.exp(log_g * (N_META - 1.0 - m[None, :, None])), 0.0)
    kdec_m = kdec_m * scale * lane
    return dict(cos=cos, sin=sin, dmat=dmat, qdec=qdec, kdec=kdec, dmat_m=dmat_m, kdec_m=kdec_m)


def _resident(tail_shape, index_map):
    return pl.BlockSpec((None,) + tail_shape, index_map, pipeline_mode=pl.Buffered(1))


def _param_specs(params, layer_index_map):
    return [_resident(a.shape[1:], functools.partial(layer_index_map, nd=a.ndim - 1)) for a in params]


def _meta_layer_map(l, nd):
    return (l,) + (0,) * nd


def _main_layer_map(s, layer, nd):
    return (layer,) + (0,) * nd


def _const_map2(*_):
    return (0, 0)


def _const_map3(*_):
    return (0, 0, 0)


def _meta_call(meta_pad, tabs, params):
    in_specs = [
        pl.BlockSpec((META_ROWS, D_MODEL), _const_map2),
        pl.BlockSpec((META_ROWS, HALF), _const_map2),
        pl.BlockSpec((META_ROWS, HALF), _const_map2),
        pl.BlockSpec((N_HEADS, META_ROWS, META_ROWS), _const_map3),
        pl.BlockSpec((N_HEADS, META_ROWS, HALF), _const_map3),
    ] + _param_specs(params, _meta_layer_map)
    out_shape = (jax.ShapeDtypeStruct((DEPTH, N_HEADS, DK, DV), F32),
                 jax.ShapeDtypeStruct((DEPTH, SUBLANES, D_MODEL), F32))
    out_specs = (pl.BlockSpec((None, N_HEADS, DK, DV), lambda l: (l, 0, 0, 0)),
                 pl.BlockSpec((None, SUBLANES, D_MODEL), lambda l: (l, 0, 0)))
    return pl.pallas_call(
        _meta_kernel,
        grid=(DEPTH,),
        in_specs=in_specs,
        out_specs=out_specs,
        out_shape=out_shape,
        scratch_shapes=[pltpu.VMEM((META_ROWS, D_MODEL), F32),
                        pltpu.VMEM((SUBLANES + META_ROWS, D_MODEL), F32)],
        compiler_params=pltpu.CompilerParams(dimension_semantics=("arbitrary",),
                                             vmem_limit_bytes=VMEM_LIMIT_BYTES),
        name="meta_tokens",
    )(meta_pad, tabs["cos"][:META_ROWS], tabs["sin"][:META_ROWS], tabs["dmat_m"], tabs["kdec_m"], *params)


def _main_call(h, layer, tabs, s_init, utail, params):
    batch, seq, _ = h.shape
    tiles_per_row = seq // TOKEN_TILE
    n_tiles = batch * tiles_per_row
    h2 = h.reshape(batch * seq, D_MODEL)
    layer_map = functools.partial(_main_layer_map, layer=layer)

    def front_tile(s):
        return (jnp.minimum(s, n_tiles - 1), 0)

    def mixer_pos(s):
        return (_rem(jnp.clip(s - 1, 0, n_tiles - 1), tiles_per_row), 0)

    def back_tile(s):
        return (jnp.clip(s - 2, 0, n_tiles - 1), 0)

    in_specs = [
        pl.BlockSpec((TOKEN_TILE, D_MODEL), front_tile),
        pl.BlockSpec((TOKEN_TILE, D_MODEL), back_tile),
        pl.BlockSpec((TOKEN_TILE, HALF), mixer_pos),
        pl.BlockSpec((TOKEN_TILE, HALF), mixer_pos),
        pl.BlockSpec((N_HEADS, SUPER, SUPER), _const_map3, pipeline_mode=pl.Buffered(1)),
        pl.BlockSpec((N_HEADS, SUPER, HALF), _const_map3, pipeline_mode=pl.Buffered(1)),
        pl.BlockSpec((N_HEADS, SUPER, HALF), _const_map3, pipeline_mode=pl.Buffered(1)),
        _resident(s_init.shape[1:], functools.partial(layer_map, nd=3)),
        _resident(utail.shape[1:], functools.partial(layer_map, nd=2)),
    ] + _param_specs(params, layer_map)
    out = pl.pallas_call(
        functools.partial(_main_kernel, tiles_per_row=tiles_per_row),
        grid=(n_tiles + PIPELINE_FILL_STEPS,),
        in_specs=in_specs,
        out_specs=pl.BlockSpec((TOKEN_TILE, D_MODEL), back_tile),
        out_shape=jax.ShapeDtypeStruct(h2.shape, F32),
        scratch_shapes=[pltpu.VMEM((N_HEADS, DK, DV), F32),
                        pltpu.VMEM((SUBLANES + TOKEN_TILE, D_MODEL), F32),
                        pltpu.VMEM((TOKEN_TILE, D_MODEL), BF16),
                        pltpu.VMEM((TOKEN_TILE, D_MODEL), F32)],
        compiler_params=pltpu.CompilerParams(dimension_semantics=("arbitrary",),
                                             vmem_limit_bytes=VMEM_LIMIT_BYTES),
        name="mixer_layer",
    )(h2, h2, tabs["cos"][N_META:], tabs["sin"][N_META:], tabs["dmat"], tabs["qdec"], tabs["kdec"],
      s_init, utail, *params)
    return out.reshape(h.shape)


def kernel(x, meta, pre_norm_g, w_in, b_in, conv_w, conv_b, w_conv_out, ret_gn_g, w_ret_out, w_o, post_norm_g):
    batch, seq, d_model = x.shape
    assert d_model == D_MODEL and seq % TOKEN_TILE == 0 and meta.shape == (N_META, D_MODEL)
    assert w_in.shape == (DEPTH, D_MODEL, D_IN)
    tabs = _tables(seq)
    params = (
        pre_norm_g.reshape(DEPTH, 1, D_MODEL),
        b_in.reshape(DEPTH, 1, D_IN),
        conv_w,
        conv_b.reshape(DEPTH, 1, D_CONV),
        ret_gn_g.reshape(DEPTH, 1, D_V),
        post_norm_g.reshape(DEPTH, 1, D_MODEL),
        _pack_rows(w_in),
        _pack_rows(w_conv_out),
        _pack_rows(w_ret_out),
        _pack_rows(w_o),
    )
    meta_pad = jnp.zeros((META_ROWS, D_MODEL), F32).at[:N_META].set(meta.astype(F32))
    s_init, utail = _meta_call(meta_pad, tabs, params)
    h = x
    for layer in range(DEPTH):
        h = _main_call(h, layer, tabs, s_init, utail, params)
    return h
```

```python
import functools

import jax
import jax.numpy as jnp
from jax import lax
from jax.experimental import pallas as pl
from jax.experimental.pallas import tpu as pltpu

D_MODEL = 1024
DEPTH = 4
CHUNK = 64
N_META = 16
CONV_WIDTH = 3
D_CONV = D_MODEL
N_HEADS = 4
DK = 256
DV = 512
D_QK = N_HEADS * DK
D_V = N_HEADS * DV
ROPE_BASE = 10000.0
RMS_EPS = 1e-6
GN_EPS = 1e-5
D_IN = 4 * D_CONV + 2 * D_QK + 2 * D_V + 2 * D_MODEL

OFF_CH = 0
OFF_CB = OFF_CH + D_CONV
OFF_CC = OFF_CB + D_CONV
OFF_CZ = OFF_CC + D_CONV
OFF_Q = OFF_CZ + D_CONV
OFF_K = OFF_Q + D_QK
OFF_V = OFF_K + D_QK
OFF_RZ = OFF_V + D_V
OFF_GA = OFF_RZ + D_V
OFF_GB = OFF_GA + D_MODEL

HALF = DK // 2
SUPER = 4 * CHUNK
TOKEN_TILE = SUPER
META_ROWS = 128
SUBLANES = 8
VMEM_LIMIT_BYTES = 60 * 1024 * 1024
ROW_PACK = 2
PACK_BLOCK_ROWS = 512
PACK_BLOCK_COLS = 2048

F32 = jnp.float32
BF16 = jnp.bfloat16


def _gammas():
    return [1.0 - 2.0 ** (-5.0 - h) for h in range(N_HEADS)]


def _pack_kernel(w_ref, o_ref):
    o_ref[...] = pltpu.bitcast(w_ref[...].astype(BF16), jnp.uint32)


def _pack_rows(w):
    layers, k, n = w.shape
    bk, bn = min(k, PACK_BLOCK_ROWS), min(n, PACK_BLOCK_COLS)
    assert k % bk == 0 and n % bn == 0
    return pl.pallas_call(
        _pack_kernel,
        grid=(layers, k // bk, n // bn),
        in_specs=[pl.BlockSpec((None, bk, bn), lambda l, i, j: (l, i, j))],
        out_specs=pl.BlockSpec((None, bk // ROW_PACK, bn), lambda l, i, j: (l, i, j)),
        out_shape=jax.ShapeDtypeStruct((layers, k // ROW_PACK, n), jnp.uint32),
        compiler_params=pltpu.CompilerParams(dimension_semantics=("arbitrary", "arbitrary", "arbitrary")),
        name="pack_weight_rows",
    )(w)


def _weight(w_ref, rows, cols):
    packed = w_ref[rows.start // ROW_PACK:rows.stop // ROW_PACK, cols]
    return pltpu.bitcast(packed, BF16)


def _dot(a, b):
    return jnp.dot(a, b, preferred_element_type=F32)


def _dot_nt(a, b):
    return lax.dot_general(a, b, (((1,), (1,)), ((), ())), preferred_element_type=F32)


def _dot_tn(a, b):
    return lax.dot_general(a, b, (((0,), (0,)), ((), ())), preferred_element_type=F32)


def _rmsnorm(x, g):
    ms = jnp.mean(x * x, axis=-1, keepdims=True)
    return x * lax.rsqrt(ms + RMS_EPS) * g


def _silu(x):
    return x * jax.nn.sigmoid(x)


def _rope(x, cos, sin):
    x1, x2 = x[:, :HALF], x[:, HALF:]
    return x1 * cos - x2 * sin, x1 * sin + x2 * cos


def _layer_rows(h, cos, sin, p, u_ref, retention):
    rows = h.shape[0]
    hg = (h * p["pre_g"][...]).astype(BF16)
    rinv = lax.rsqrt(jnp.mean(h * h, axis=-1, keepdims=True) + RMS_EPS)

    def proj(off, width):
        w = _weight(p["w_in"], slice(0, D_MODEL), slice(off, off + width))
        return _dot(hg, w) * rinv + p["b_in"][:, off:off + width]

    def head_proj(hd):
        return (proj(OFF_Q + hd * DK, DK), proj(OFF_K + hd * DK, DK),
                proj(OFF_V + hd * DV, DV), proj(OFF_RZ + hd * DV, DV))

    def head_rope(raw):
        q, k, v, z = raw
        return _rope(q, cos, sin), _rope(k, cos, sin), v.astype(BF16), z

    def head_norm(hd, o, z):
        mu = jnp.mean(o, axis=-1, keepdims=True)
        oc = o - mu
        var = jnp.mean(oc * oc, axis=-1, keepdims=True)
        on = oc * lax.rsqrt(var + GN_EPS) * p["gn_g"][:, hd * DV:(hd + 1) * DV]
        return (on * _silu(z)).astype(BF16)

    def head_out(hd, on):
        return _dot(on, _weight(p["w_ro"], slice(hd * DV, (hd + 1) * DV), slice(None)))

    raw = head_proj(0)
    y_b = None
    pending = None
    for hd in range(N_HEADS):
        q, k, v, z = head_rope(raw)
        on = None if pending is None else head_norm(*pending)
        if hd + 1 < N_HEADS:
            raw = head_proj(hd + 1)
        else:
            c_c, c_h = proj(OFF_CC, D_CONV), proj(OFF_CH, D_CONV)
        if on is not None:
            y_hd = head_out(pending[0], on)
            y_b = y_hd if y_b is None else y_b + y_hd
        pending = (hd, retention(hd, q, k, v), z)

    on = head_norm(*pending)
    u = c_c * c_h
    u_ref[SUBLANES:SUBLANES + rows, :] = u
    conv = p["conv_b"][...] + p["conv_w"][2:3, :] * u
    conv = conv + p["conv_w"][1:2, :] * u_ref[SUBLANES - 1:SUBLANES - 1 + rows, :]
    conv = conv + p["conv_w"][0:1, :] * u_ref[SUBLANES - 2:SUBLANES - 2 + rows, :]
    c_b, c_z = proj(OFF_CB, D_CONV), proj(OFF_CZ, D_CONV)
    y_b = y_b + head_out(pending[0], on)
    y = (c_b * conv * _silu(c_z)).astype(BF16)
    g_a, g_b = proj(OFF_GA, D_MODEL), proj(OFF_GB, D_MODEL)
    y_a = _dot(y, _weight(p["w_co"], slice(0, D_CONV), slice(None)))
    mix = jax.nn.sigmoid(g_a) * y_a + jax.nn.sigmoid(g_b) * y_b
    y_o = _dot(mix.astype(BF16), _weight(p["w_o"], slice(0, D_MODEL), slice(None)))
    return h + _rmsnorm(y_o, p["post_g"][...])


def _cat(halves):
    return jnp.concatenate(halves, axis=1)


def _main_kernel(h_ref, cos_ref, sin_ref, dmat_ref, qdec_ref, kdec_ref, s_init_ref, utail_ref,
                 pre_g, b_in, conv_w, conv_b, gn_g, post_g, w_in, w_co, w_ro, w_o,
                 out_ref, state_ref, u_ref):
    rows = h_ref.shape[0]

    @pl.when(pl.program_id(1) == 0)
    def _():
        state_ref[...] = s_init_ref[...]
        u_ref[0:SUBLANES, :] = utail_ref[...]

    gammas = _gammas()

    def retention(hd, q, k, v):
        outs = []
        for c in range(rows // SUPER):
            r = slice(c * SUPER, (c + 1) * SUPER)
            q1, q2 = q[0][r], q[1][r]
            k1, k2 = k[0][r], k[1][r]
            vc = v[r]
            qdec = qdec_ref[hd]
            kdec = kdec_ref[hd]
            qr = _cat([q1, q2]).astype(BF16)
            kr = _cat([k1, k2]).astype(BF16)
            qd = _cat([q1 * qdec, q2 * qdec]).astype(BF16)
            kd = _cat([k1 * kdec, k2 * kdec]).astype(BF16)
            s = _dot_nt(qr, kr)
            state = state_ref[hd]
            cross = _dot(qd, state.astype(BF16))
            state_ref[hd] = (gammas[hd] ** SUPER) * state + _dot_tn(kd, vc)
            outs.append(_dot((s * dmat_ref[hd]).astype(BF16), vc) + cross)
        return outs[0] if len(outs) == 1 else jnp.concatenate(outs, axis=0)

    p = dict(pre_g=pre_g, b_in=b_in, conv_w=conv_w, conv_b=conv_b, gn_g=gn_g, post_g=post_g,
             w_in=w_in, w_co=w_co, w_ro=w_ro, w_o=w_o)
    out_ref[...] = _layer_rows(h_ref[...], cos_ref[...], sin_ref[...], p, u_ref, retention)
    u_ref[0:SUBLANES, :] = u_ref[rows:rows + SUBLANES, :]


def _meta_kernel(meta_ref, cos_ref, sin_ref, dmat_ref, kdec_ref,
                 pre_g, b_in, conv_w, conv_b, gn_g, post_g, w_in, w_co, w_ro, w_o,
                 s_out_ref, utail_ref, hm_ref, u_ref):
    @pl.when(pl.program_id(0) == 0)
    def _():
        hm_ref[...] = meta_ref[...]
        u_ref[0:SUBLANES, :] = jnp.zeros((SUBLANES, D_MODEL), F32)

    def retention(hd, q, k, v):
        kdec = kdec_ref[hd]
        qr = _cat(q).astype(BF16)
        kr = _cat(k).astype(BF16)
        kd = _cat([k[0] * kdec, k[1] * kdec]).astype(BF16)
        s = (_dot_nt(qr, kr) * dmat_ref[hd]).astype(BF16)
        s_out_ref[hd] = _dot_tn(kd, v)
        return _dot(s, v)

    p = dict(pre_g=pre_g, b_in=b_in, conv_w=conv_w, conv_b=conv_b, gn_g=gn_g, post_g=post_g,
             w_in=w_in, w_co=w_co, w_ro=w_ro, w_o=w_o)
    hm_ref[...] = _layer_rows(hm_ref[...], cos_ref[...], sin_ref[...], p, u_ref, retention)
    utail_ref[...] = u_ref[N_META:N_META + SUBLANES, :]


def _tables(seq):
    total = N_META + seq
    pos = jnp.arange(total, dtype=F32)
    inv = ROPE_BASE ** (-jnp.arange(HALF, dtype=F32) / HALF)
    ang = pos[:, None] * inv[None, :]
    cos, sin = jnp.cos(ang), jnp.sin(ang)

    log_g = jnp.log(1.0 - 2.0 ** (-5.0 - jnp.arange(N_HEADS, dtype=F32)))[:, None, None]
    scale = DK ** -0.5
    i = jnp.arange(SUPER, dtype=F32)
    dist = jnp.abs(i[:, None] - i[None, :])
    visible = (jnp.arange(SUPER)[None, :] // CHUNK) <= (jnp.arange(SUPER)[:, None] // CHUNK)
    dmat = jnp.where(visible[None], jnp.exp(log_g * dist[None]), 0.0) * scale
    lane = jnp.ones((1, 1, HALF), F32)
    qdec = jnp.exp(log_g * (i[None, :, None] + 1.0)) * lane
    kdec = jnp.exp(log_g * (SUPER - 1.0 - i[None, :, None])) * scale * lane

    m = jnp.arange(META_ROWS, dtype=F32)
    is_meta = jnp.arange(META_ROWS) < N_META
    mdist = jnp.abs(m[:, None] - m[None, :])
    dmat_m = jnp.where((is_meta[:, None] & is_meta[None, :])[None], jnp.exp(log_g * mdist[None]), 0.0) * scale
    kdec_m = jnp.where(is_meta[None, :, None], jnp.exp(log_g * (N_META - 1.0 - m[None, :, None])), 0.0)
    kdec_m = kdec_m * scale * lane
    return dict(cos=cos, sin=sin, dmat=dmat, qdec=qdec, kdec=kdec, dmat_m=dmat_m, kdec_m=kdec_m)


def _resident(tail_shape, index_map):
    return pl.BlockSpec((None,) + tail_shape, index_map, pipeline_mode=pl.Buffered(1))


def _param_specs(params, layer_index_map):
    return [_resident(a.shape[1:], functools.partial(layer_index_map, nd=a.ndim - 1)) for a in params]


def _meta_layer_map(l, nd):
    return (l,) + (0,) * nd


def _main_layer_map(b, t, layer, nd):
    return (layer,) + (0,) * nd


def _const_map2(*_):
    return (0, 0)


def _const_map3(*_):
    return (0, 0, 0)


def _meta_call(meta_pad, tabs, params):
    in_specs = [
        pl.BlockSpec((META_ROWS, D_MODEL), _const_map2),
        pl.BlockSpec((META_ROWS, HALF), _const_map2),
        pl.BlockSpec((META_ROWS, HALF), _const_map2),
        pl.BlockSpec((N_HEADS, META_ROWS, META_ROWS), _const_map3),
        pl.BlockSpec((N_HEADS, META_ROWS, HALF), _const_map3),
    ] + _param_specs(params, _meta_layer_map)
    out_shape = (jax.ShapeDtypeStruct((DEPTH, N_HEADS, DK, DV), F32),
                 jax.ShapeDtypeStruct((DEPTH, SUBLANES, D_MODEL), F32))
    out_specs = (pl.BlockSpec((None, N_HEADS, DK, DV), lambda l: (l, 0, 0, 0)),
                 pl.BlockSpec((None, SUBLANES, D_MODEL), lambda l: (l, 0, 0)))
    return pl.pallas_call(
        _meta_kernel,
        grid=(DEPTH,),
        in_specs=in_specs,
        out_specs=out_specs,
        out_shape=out_shape,
        scratch_shapes=[pltpu.VMEM((META_ROWS, D_MODEL), F32),
                        pltpu.VMEM((SUBLANES + META_ROWS, D_MODEL), F32)],
        compiler_params=pltpu.CompilerParams(dimension_semantics=("arbitrary",),
                                             vmem_limit_bytes=VMEM_LIMIT_BYTES),
        name="meta_tokens",
    )(meta_pad, tabs["cos"][:META_ROWS], tabs["sin"][:META_ROWS], tabs["dmat_m"], tabs["kdec_m"], *params)


def _main_call(h, layer, tabs, s_init, utail, params):
    batch, seq, _ = h.shape
    layer_map = functools.partial(_main_layer_map, layer=layer)
    in_specs = [
        pl.BlockSpec((None, TOKEN_TILE, D_MODEL), lambda b, t: (b, t, 0)),
        pl.BlockSpec((TOKEN_TILE, HALF), lambda b, t: (t, 0)),
        pl.BlockSpec((TOKEN_TILE, HALF), lambda b, t: (t, 0)),
        pl.BlockSpec((N_HEADS, SUPER, SUPER), _const_map3, pipeline_mode=pl.Buffered(1)),
        pl.BlockSpec((N_HEADS, SUPER, HALF), _const_map3, pipeline_mode=pl.Buffered(1)),
        pl.BlockSpec((N_HEADS, SUPER, HALF), _const_map3, pipeline_mode=pl.Buffered(1)),
        _resident(s_init.shape[1:], functools.partial(layer_map, nd=3)),
        _resident(utail.shape[1:], functools.partial(layer_map, nd=2)),
    ] + _param_specs(params, layer_map)
    return pl.pallas_call(
        _main_kernel,
        grid=(batch, seq // TOKEN_TILE),
        in_specs=in_specs,
        out_specs=pl.BlockSpec((None, TOKEN_TILE, D_MODEL), lambda b, t: (b, t, 0)),
        out_shape=jax.ShapeDtypeStruct(h.shape, F32),
        scratch_shapes=[pltpu.VMEM((N_HEADS, DK, DV), F32),
                        pltpu.VMEM((SUBLANES + TOKEN_TILE, D_MODEL), F32)],
        compiler_params=pltpu.CompilerParams(dimension_semantics=("arbitrary", "arbitrary"),
                                             vmem_limit_bytes=VMEM_LIMIT_BYTES),
        name="mixer_layer",
    )(h, tabs["cos"][N_META:], tabs["sin"][N_META:], tabs["dmat"], tabs["qdec"], tabs["kdec"],
      s_init, utail, *params)


def kernel(x, meta, pre_norm_g, w_in, b_in, conv_w, conv_b, w_conv_out, ret_gn_g, w_ret_out, w_o, post_norm_g):
    batch, seq, d_model = x.shape
    assert d_model == D_MODEL and seq % TOKEN_TILE == 0 and meta.shape == (N_META, D_MODEL)
    assert w_in.shape == (DEPTH, D_MODEL, D_IN)
    tabs = _tables(seq)
    params = (
        pre_norm_g.reshape(DEPTH, 1, D_MODEL),
        b_in.reshape(DEPTH, 1, D_IN),
        conv_w,
        conv_b.reshape(DEPTH, 1, D_CONV),
        ret_gn_g.reshape(DEPTH, 1, D_V),
        post_norm_g.reshape(DEPTH, 1, D_MODEL),
        _pack_rows(w_in),
        _pack_rows(w_conv_out),
        _pack_rows(w_ret_out),
        _pack_rows(w_o),
    )
    meta_pad = jnp.zeros((META_ROWS, D_MODEL), F32).at[:N_META].set(meta.astype(F32))
    s_init, utail = _meta_call(meta_pad, tabs, params)
    h = x
    for layer in range(DEPTH):
        h = _main_call(h, layer, tabs, s_init, utail, params)
    return h
```

```python
import functools

import jax
import jax.numpy as jnp
from jax import lax
from jax.experimental import pallas as pl
from jax.experimental.pallas import tpu as pltpu

D_MODEL = 1024
DEPTH = 4
CHUNK = 64
N_META = 16
CONV_WIDTH = 3
D_CONV = D_MODEL
N_HEADS = 4
DK = 256
DV = 512
D_QK = N_HEADS * DK
D_V = N_HEADS * DV
ROPE_BASE = 10000.0
RMS_EPS = 1e-6
GN_EPS = 1e-5
D_IN = 4 * D_CONV + 2 * D_QK + 2 * D_V + 2 * D_MODEL

OFF_CH = 0
OFF_CB = OFF_CH + D_CONV
OFF_CC = OFF_CB + D_CONV
OFF_CZ = OFF_CC + D_CONV
OFF_Q = OFF_CZ + D_CONV
OFF_K = OFF_Q + D_QK
OFF_V = OFF_K + D_QK
OFF_RZ = OFF_V + D_V
OFF_GA = OFF_RZ + D_V
OFF_GB = OFF_GA + D_MODEL

HALF = DK // 2
SUPER = 4 * CHUNK
TOKEN_TILE = 2 * SUPER
META_ROWS = 128
SUBLANES = 8
VMEM_LIMIT_BYTES = 63 * 1024 * 1024
ROW_PACK = 2
PACK_BLOCK_ROWS = 512
PACK_BLOCK_COLS = 2048

F32 = jnp.float32
BF16 = jnp.bfloat16


def _gammas():
    return [1.0 - 2.0 ** (-5.0 - h) for h in range(N_HEADS)]


def _pack_kernel(w_ref, o_ref):
    o_ref[...] = pltpu.bitcast(w_ref[...].astype(BF16), jnp.uint32)


def _pack_rows(w):
    layers, k, n = w.shape
    bk, bn = min(k, PACK_BLOCK_ROWS), min(n, PACK_BLOCK_COLS)
    assert k % bk == 0 and n % bn == 0
    return pl.pallas_call(
        _pack_kernel,
        grid=(layers, k // bk, n // bn),
        in_specs=[pl.BlockSpec((None, bk, bn), lambda l, i, j: (l, i, j))],
        out_specs=pl.BlockSpec((None, bk // ROW_PACK, bn), lambda l, i, j: (l, i, j)),
        out_shape=jax.ShapeDtypeStruct((layers, k // ROW_PACK, n), jnp.uint32),
        compiler_params=pltpu.CompilerParams(dimension_semantics=("arbitrary", "arbitrary", "arbitrary")),
        name="pack_weight_rows",
    )(w)


def _weight(w_ref, rows, cols):
    packed = w_ref[rows.start // ROW_PACK:rows.stop // ROW_PACK, cols]
    return pltpu.bitcast(packed, BF16)


def _dot(a, b):
    return jnp.dot(a, b, preferred_element_type=F32)


def _dot_nt(a, b):
    return lax.dot_general(a, b, (((1,), (1,)), ((), ())), preferred_element_type=F32)


def _dot_tn(a, b):
    return lax.dot_general(a, b, (((0,), (0,)), ((), ())), preferred_element_type=F32)


def _rmsnorm(x, g):
    ms = jnp.mean(x * x, axis=-1, keepdims=True)
    return x * lax.rsqrt(ms + RMS_EPS) * g


def _silu(x):
    return x * jax.nn.sigmoid(x)


def _rope(x, cos, sin):
    x1, x2 = x[:, :HALF], x[:, HALF:]
    return x1 * cos - x2 * sin, x1 * sin + x2 * cos


def _layer_rows(h, cos, sin, p, u_ref, retention):
    rows = h.shape[0]
    xn = _rmsnorm(h, p["pre_g"][...]).astype(BF16)

    def proj(off, width):
        w = _weight(p["w_in"], slice(0, D_MODEL), slice(off, off + width))
        return _dot(xn, w) + p["b_in"][:, off:off + width]

    def head_proj(hd):
        return (proj(OFF_Q + hd * DK, DK), proj(OFF_K + hd * DK, DK),
                proj(OFF_V + hd * DV, DV), proj(OFF_RZ + hd * DV, DV))

    def head_rope(raw):
        q, k, v, z = raw
        return _rope(q, cos, sin), _rope(k, cos, sin), v.astype(BF16), z

    def head_norm(hd, o, z):
        mu = jnp.mean(o, axis=-1, keepdims=True)
        oc = o - mu
        var = jnp.mean(oc * oc, axis=-1, keepdims=True)
        on = oc * lax.rsqrt(var + GN_EPS) * p["gn_g"][:, hd * DV:(hd + 1) * DV]
        return (on * _silu(z)).astype(BF16)

    def head_out(hd, on):
        return _dot(on, _weight(p["w_ro"], slice(hd * DV, (hd + 1) * DV), slice(None)))

    raw = head_proj(0)
    y_b = None
    pending = None
    for hd in range(N_HEADS):
        q, k, v, z = head_rope(raw)
        on = None if pending is None else head_norm(*pending)
        if hd + 1 < N_HEADS:
            raw = head_proj(hd + 1)
        else:
            c_c, c_h = proj(OFF_CC, D_CONV), proj(OFF_CH, D_CONV)
        if on is not None:
            y_hd = head_out(pending[0], on)
            y_b = y_hd if y_b is None else y_b + y_hd
        pending = (hd, retention(hd, q, k, v), z)

    on = head_norm(*pending)
    u = c_c * c_h
    u_ref[SUBLANES:SUBLANES + rows, :] = u
    conv = p["conv_b"][...] + p["conv_w"][2:3, :] * u
    conv = conv + p["conv_w"][1:2, :] * u_ref[SUBLANES - 1:SUBLANES - 1 + rows, :]
    conv = conv + p["conv_w"][0:1, :] * u_ref[SUBLANES - 2:SUBLANES - 2 + rows, :]
    c_b, c_z = proj(OFF_CB, D_CONV), proj(OFF_CZ, D_CONV)
    y_b = y_b + head_out(pending[0], on)
    y = (c_b * conv * _silu(c_z)).astype(BF16)
    g_a, g_b = proj(OFF_GA, D_MODEL), proj(OFF_GB, D_MODEL)
    y_a = _dot(y, _weight(p["w_co"], slice(0, D_CONV), slice(None)))
    mix = jax.nn.sigmoid(g_a) * y_a + jax.nn.sigmoid(g_b) * y_b
    y_o = _dot(mix.astype(BF16), _weight(p["w_o"], slice(0, D_MODEL), slice(None)))
    return h + _rmsnorm(y_o, p["post_g"][...])


def _cat(halves):
    return jnp.concatenate(halves, axis=1)


def _main_kernel(h_ref, cos_ref, sin_ref, dmat_ref, qdec_ref, kdec_ref, s_init_ref, utail_ref,
                 pre_g, b_in, conv_w, conv_b, gn_g, post_g, w_in, w_co, w_ro, w_o,
                 out_ref, state_ref, u_ref):
    rows = h_ref.shape[0]

    @pl.when(pl.program_id(1) == 0)
    def _():
        state_ref[...] = s_init_ref[...]
        u_ref[0:SUBLANES, :] = utail_ref[...]

    gammas = _gammas()

    def retention(hd, q, k, v):
        outs = []
        for c in range(rows // SUPER):
            r = slice(c * SUPER, (c + 1) * SUPER)
            q1, q2 = q[0][r], q[1][r]
            k1, k2 = k[0][r], k[1][r]
            vc = v[r]
            qdec = qdec_ref[hd]
            kdec = kdec_ref[hd]
            qr = _cat([q1, q2]).astype(BF16)
            kr = _cat([k1, k2]).astype(BF16)
            qd = _cat([q1 * qdec, q2 * qdec]).astype(BF16)
            kd = _cat([k1 * kdec, k2 * kdec]).astype(BF16)
            s = _dot_nt(qr, kr)
            state = state_ref[hd]
            cross = _dot(qd, state.astype(BF16))
            state_ref[hd] = (gammas[hd] ** SUPER) * state + _dot_tn(kd, vc)
            outs.append(_dot((s * dmat_ref[hd]).astype(BF16), vc) + cross)
        return outs[0] if len(outs) == 1 else jnp.concatenate(outs, axis=0)

    p = dict(pre_g=pre_g, b_in=b_in, conv_w=conv_w, conv_b=conv_b, gn_g=gn_g, post_g=post_g,
             w_in=w_in, w_co=w_co, w_ro=w_ro, w_o=w_o)
    out_ref[...] = _layer_rows(h_ref[...], cos_ref[...], sin_ref[...], p, u_ref, retention)
    u_ref[0:SUBLANES, :] = u_ref[rows:rows + SUBLANES, :]


def _meta_kernel(meta_ref, cos_ref, sin_ref, dmat_ref, kdec_ref,
                 pre_g, b_in, conv_w, conv_b, gn_g, post_g, w_in, w_co, w_ro, w_o,
                 s_out_ref, utail_ref, hm_ref, u_ref):
    @pl.when(pl.program_id(0) == 0)
    def _():
        hm_ref[...] = meta_ref[...]
        u_ref[0:SUBLANES, :] = jnp.zeros((SUBLANES, D_MODEL), F32)

    def retention(hd, q, k, v):
        kdec = kdec_ref[hd]
        qr = _cat(q).astype(BF16)
        kr = _cat(k).astype(BF16)
        kd = _cat([k[0] * kdec, k[1] * kdec]).astype(BF16)
        s = (_dot_nt(qr, kr) * dmat_ref[hd]).astype(BF16)
        s_out_ref[hd] = _dot_tn(kd, v)
        return _dot(s, v)

    p = dict(pre_g=pre_g, b_in=b_in, conv_w=conv_w, conv_b=conv_b, gn_g=gn_g, post_g=post_g,
             w_in=w_in, w_co=w_co, w_ro=w_ro, w_o=w_o)
    hm_ref[...] = _layer_rows(hm_ref[...], cos_ref[...], sin_ref[...], p, u_ref, retention)
    utail_ref[...] = u_ref[N_META:N_META + SUBLANES, :]


def _tables(seq):
    total = N_META + seq
    pos = jnp.arange(total, dtype=F32)
    inv = ROPE_BASE ** (-jnp.arange(HALF, dtype=F32) / HALF)
    ang = pos[:, None] * inv[None, :]
    cos, sin = jnp.cos(ang), jnp.sin(ang)

    log_g = jnp.log(1.0 - 2.0 ** (-5.0 - jnp.arange(N_HEADS, dtype=F32)))[:, None, None]
    scale = DK ** -0.5
    i = jnp.arange(SUPER, dtype=F32)
    dist = jnp.abs(i[:, None] - i[None, :])
    visible = (jnp.arange(SUPER)[None, :] // CHUNK) <= (jnp.arange(SUPER)[:, None] // CHUNK)
    dmat = jnp.where(visible[None], jnp.exp(log_g * dist[None]), 0.0) * scale
    lane = jnp.ones((1, 1, HALF), F32)
    qdec = jnp.exp(log_g * (i[None, :, None] + 1.0)) * lane
    kdec = jnp.exp(log_g * (SUPER - 1.0 - i[None, :, None])) * scale * lane

    m = jnp.arange(META_ROWS, dtype=F32)
    is_meta = jnp.arange(META_ROWS) < N_META
    mdist = jnp.abs(m[:, None] - m[None, :])
    dmat_m = jnp.where((is_meta[:, None] & is_meta[None, :])[None], jnp.exp(log_g * mdist[None]), 0.0) * scale
    kdec_m = jnp.where(is_meta[None, :, None], jnp.exp(log_g * (N_META - 1.0 - m[None, :, None])), 0.0)
    kdec_m = kdec_m * scale * lane
    return dict(cos=cos, sin=sin, dmat=dmat, qdec=qdec, kdec=kdec, dmat_m=dmat_m, kdec_m=kdec_m)


def _resident(tail_shape, index_map):
    return pl.BlockSpec((None,) + tail_shape, index_map, pipeline_mode=pl.Buffered(1))


def _param_specs(params, layer_index_map):
    return [_resident(a.shape[1:], functools.partial(layer_index_map, nd=a.ndim - 1)) for a in params]


def _meta_layer_map(l, nd):
    return (l,) + (0,) * nd


def _main_layer_map(b, t, layer, nd):
    return (layer,) + (0,) * nd


def _const_map2(*_):
    return (0, 0)


def _const_map3(*_):
    return (0, 0, 0)


def _meta_call(meta_pad, tabs, params):
    in_specs = [
        pl.BlockSpec((META_ROWS, D_MODEL), _const_map2),
        pl.BlockSpec((META_ROWS, HALF), _const_map2),
        pl.BlockSpec((META_ROWS, HALF), _const_map2),
        pl.BlockSpec((N_HEADS, META_ROWS, META_ROWS), _const_map3),
        pl.BlockSpec((N_HEADS, META_ROWS, HALF), _const_map3),
    ] + _param_specs(params, _meta_layer_map)
    out_shape = (jax.ShapeDtypeStruct((DEPTH, N_HEADS, DK, DV), F32),
                 jax.ShapeDtypeStruct((DEPTH, SUBLANES, D_MODEL), F32))
    out_specs = (pl.BlockSpec((None, N_HEADS, DK, DV), lambda l: (l, 0, 0, 0)),
                 pl.BlockSpec((None, SUBLANES, D_MODEL), lambda l: (l, 0, 0)))
    return pl.pallas_call(
        _meta_kernel,
        grid=(DEPTH,),
        in_specs=in_specs,
        out_specs=out_specs,
        out_shape=out_shape,
        scratch_shapes=[pltpu.VMEM((META_ROWS, D_MODEL), F32),
                        pltpu.VMEM((SUBLANES + META_ROWS, D_MODEL), F32)],
        compiler_params=pltpu.CompilerParams(dimension_semantics=("arbitrary",),
                                             vmem_limit_bytes=VMEM_LIMIT_BYTES),
        name="meta_tokens",
    )(meta_pad, tabs["cos"][:META_ROWS], tabs["sin"][:META_ROWS], tabs["dmat_m"], tabs["kdec_m"], *params)


def _main_call(h, layer, tabs, s_init, utail, params):
    batch, seq, _ = h.shape
    layer_map = functools.partial(_main_layer_map, layer=layer)
    in_specs = [
        pl.BlockSpec((None, TOKEN_TILE, D_MODEL), lambda b, t: (b, t, 0)),
        pl.BlockSpec((TOKEN_TILE, HALF), lambda b, t: (t, 0)),
        pl.BlockSpec((TOKEN_TILE, HALF), lambda b, t: (t, 0)),
        pl.BlockSpec((N_HEADS, SUPER, SUPER), _const_map3, pipeline_mode=pl.Buffered(1)),
        pl.BlockSpec((N_HEADS, SUPER, HALF), _const_map3, pipeline_mode=pl.Buffered(1)),
        pl.BlockSpec((N_HEADS, SUPER, HALF), _const_map3, pipeline_mode=pl.Buffered(1)),
        _resident(s_init.shape[1:], functools.partial(layer_map, nd=3)),
        _resident(utail.shape[1:], functools.partial(layer_map, nd=2)),
    ] + _param_specs(params, layer_map)
    return pl.pallas_call(
        _main_kernel,
        grid=(batch, seq // TOKEN_TILE),
        in_specs=in_specs,
        out_specs=pl.BlockSpec((None, TOKEN_TILE, D_MODEL), lambda b, t: (b, t, 0)),
        out_shape=jax.ShapeDtypeStruct(h.shape, F32),
        scratch_shapes=[pltpu.VMEM((N_HEADS, DK, DV), F32),
                        pltpu.VMEM((SUBLANES + TOKEN_TILE, D_MODEL), F32)],
        compiler_params=pltpu.CompilerParams(dimension_semantics=("arbitrary", "arbitrary"),
                                             vmem_limit_bytes=VMEM_LIMIT_BYTES),
        name="mixer_layer",
    )(h, tabs["cos"][N_META:], tabs["sin"][N_META:], tabs["dmat"], tabs["qdec"], tabs["kdec"],
      s_init, utail, *params)


def kernel(x, meta, pre_norm_g, w_in, b_in, conv_w, conv_b, w_conv_out, ret_gn_g, w_ret_out, w_o, post_norm_g):
    batch, seq, d_model = x.shape
    assert d_model == D_MODEL and seq % TOKEN_TILE == 0 and meta.shape == (N_META, D_MODEL)
    assert w_in.shape == (DEPTH, D_MODEL, D_IN)
    tabs = _tables(seq)
    params = (
        pre_norm_g.reshape(DEPTH, 1, D_MODEL),
        b_in.reshape(DEPTH, 1, D_IN),
        conv_w,
        conv_b.reshape(DEPTH, 1, D_CONV),
        ret_gn_g.reshape(DEPTH, 1, D_V),
        post_norm_g.reshape(DEPTH, 1, D_MODEL),
        _pack_rows(w_in),
        _pack_rows(w_conv_out),
        _pack_rows(w_ret_out),
        _pack_rows(w_o),
    )
    meta_pad = jnp.zeros((META_ROWS, D_MODEL), F32).at[:N_META].set(meta.astype(F32))
    s_init, utail = _meta_call(meta_pad, tabs, params)
    h = x
    for layer in range(DEPTH):
        h = _main_call(h, layer, tabs, s_init, utail, params)
    return h
```

```python
import functools

import jax
import jax.numpy as jnp
from jax import lax
from jax.experimental import pallas as pl
from jax.experimental.pallas import tpu as pltpu

D_MODEL = 1024
DEPTH = 4
CHUNK = 64
N_META = 16
CONV_WIDTH = 3
D_CONV = D_MODEL
N_HEADS = 4
DK = 256
DV = 512
D_QK = N_HEADS * DK
D_V = N_HEADS * DV
ROPE_BASE = 10000.0
RMS_EPS = 1e-6
GN_EPS = 1e-5
D_IN = 4 * D_CONV + 2 * D_QK + 2 * D_V + 2 * D_MODEL

OFF_CH = 0
OFF_CB = OFF_CH + D_CONV
OFF_CC = OFF_CB + D_CONV
OFF_CZ = OFF_CC + D_CONV
OFF_Q = OFF_CZ + D_CONV
OFF_K = OFF_Q + D_QK
OFF_V = OFF_K + D_QK
OFF_RZ = OFF_V + D_V
OFF_GA = OFF_RZ + D_V
OFF_GB = OFF_GA + D_MODEL

HALF = DK // 2
SUPER = 4 * CHUNK
SUB_TILE = SUPER
TOKEN_TILE = 2 * SUB_TILE
META_ROWS = 128
SUBLANES = 8
VMEM_LIMIT_BYTES = 60 * 1024 * 1024
ROW_PACK = 2
PACK_BLOCK_ROWS = 512
PACK_BLOCK_COLS = 2048

F32 = jnp.float32
BF16 = jnp.bfloat16


def _gammas():
    return [1.0 - 2.0 ** (-5.0 - h) for h in range(N_HEADS)]


def _pack_kernel(w_ref, o_ref):
    o_ref[...] = pltpu.bitcast(w_ref[...].astype(BF16), jnp.uint32)


def _pack_rows(w):
    layers, k, n = w.shape
    bk, bn = min(k, PACK_BLOCK_ROWS), min(n, PACK_BLOCK_COLS)
    assert k % bk == 0 and n % bn == 0
    return pl.pallas_call(
        _pack_kernel,
        grid=(layers, k // bk, n // bn),
        in_specs=[pl.BlockSpec((None, bk, bn), lambda l, i, j: (l, i, j))],
        out_specs=pl.BlockSpec((None, bk // ROW_PACK, bn), lambda l, i, j: (l, i, j)),
        out_shape=jax.ShapeDtypeStruct((layers, k // ROW_PACK, n), jnp.uint32),
        compiler_params=pltpu.CompilerParams(dimension_semantics=("arbitrary", "arbitrary", "arbitrary")),
        name="pack_weight_rows",
    )(w)


def _weight(w_ref, rows, cols):
    packed = w_ref[rows.start // ROW_PACK:rows.stop // ROW_PACK, cols]
    return pltpu.bitcast(packed, BF16)


def _dot(a, b):
    return jnp.dot(a, b, preferred_element_type=F32)


def _dot_nt(a, b):
    return lax.dot_general(a, b, (((1,), (1,)), ((), ())), preferred_element_type=F32)


def _dot_tn(a, b):
    return lax.dot_general(a, b, (((0,), (0,)), ((), ())), preferred_element_type=F32)


def _rmsnorm(x, g):
    ms = jnp.mean(x * x, axis=-1, keepdims=True)
    return x * lax.rsqrt(ms + RMS_EPS) * g


def _silu(x):
    return x * jax.nn.sigmoid(x)


def _rope(x, cos, sin):
    x1, x2 = x[:, :HALF], x[:, HALF:]
    return x1 * cos - x2 * sin, x1 * sin + x2 * cos


def _layer_rows(h, cos, sin, p, u_ref, retention):
    rows = h.shape[0]
    xn = _rmsnorm(h, p["pre_g"][...]).astype(BF16)

    def proj(off, width):
        w = _weight(p["w_in"], slice(0, D_MODEL), slice(off, off + width))
        return _dot(xn, w) + p["b_in"][:, off:off + width]

    def head_proj(hd):
        return (proj(OFF_Q + hd * DK, DK), proj(OFF_K + hd * DK, DK),
                proj(OFF_V + hd * DV, DV), proj(OFF_RZ + hd * DV, DV))

    def head_rope(raw):
        q, k, v, z = raw
        return _rope(q, cos, sin), _rope(k, cos, sin), v.astype(BF16), z

    def head_norm(hd, o, z):
        mu = jnp.mean(o, axis=-1, keepdims=True)
        oc = o - mu
        var = jnp.mean(oc * oc, axis=-1, keepdims=True)
        on = oc * lax.rsqrt(var + GN_EPS) * p["gn_g"][:, hd * DV:(hd + 1) * DV]
        return (on * _silu(z)).astype(BF16)

    def head_out(hd, on):
        return _dot(on, _weight(p["w_ro"], slice(hd * DV, (hd + 1) * DV), slice(None)))

    raw = head_proj(0)
    y_b = None
    pending = None
    for hd in range(N_HEADS):
        q, k, v, z = head_rope(raw)
        on = None if pending is None else head_norm(*pending)
        if hd + 1 < N_HEADS:
            raw = head_proj(hd + 1)
        else:
            c_c, c_h = proj(OFF_CC, D_CONV), proj(OFF_CH, D_CONV)
        if on is not None:
            y_hd = head_out(pending[0], on)
            y_b = y_hd if y_b is None else y_b + y_hd
        pending = (hd, retention(hd, q, k, v), z)

    on = head_norm(*pending)
    u = c_c * c_h
    u_ref[SUBLANES:SUBLANES + rows, :] = u
    conv = p["conv_b"][...] + p["conv_w"][2:3, :] * u
    conv = conv + p["conv_w"][1:2, :] * u_ref[SUBLANES - 1:SUBLANES - 1 + rows, :]
    conv = conv + p["conv_w"][0:1, :] * u_ref[SUBLANES - 2:SUBLANES - 2 + rows, :]
    c_b, c_z = proj(OFF_CB, D_CONV), proj(OFF_CZ, D_CONV)
    y_b = y_b + head_out(pending[0], on)
    y = (c_b * conv * _silu(c_z)).astype(BF16)
    g_a, g_b = proj(OFF_GA, D_MODEL), proj(OFF_GB, D_MODEL)
    y_a = _dot(y, _weight(p["w_co"], slice(0, D_CONV), slice(None)))
    mix = jax.nn.sigmoid(g_a) * y_a + jax.nn.sigmoid(g_b) * y_b
    y_o = _dot(mix.astype(BF16), _weight(p["w_o"], slice(0, D_MODEL), slice(None)))
    return h + _rmsnorm(y_o, p["post_g"][...])


def _cat(halves):
    return jnp.concatenate(halves, axis=1)


def _main_kernel(h_ref, cos_ref, sin_ref, dmat_ref, qdec_ref, kdec_ref, s_init_ref, utail_ref,
                 pre_g, b_in, conv_w, conv_b, gn_g, post_g, w_in, w_co, w_ro, w_o,
                 out_ref, state_ref, u_ref):
    rows = h_ref.shape[0]

    @pl.when(pl.program_id(1) == 0)
    def _():
        state_ref[...] = s_init_ref[...]
        u_ref[0:SUBLANES, :] = utail_ref[...]

    gammas = _gammas()

    def retention(hd, q, k, v):
        outs = []
        for c in range(v.shape[0] // SUPER):
            r = slice(c * SUPER, (c + 1) * SUPER)
            q1, q2 = q[0][r], q[1][r]
            k1, k2 = k[0][r], k[1][r]
            vc = v[r]
            qdec = qdec_ref[hd]
            kdec = kdec_ref[hd]
            qr = _cat([q1, q2]).astype(BF16)
            kr = _cat([k1, k2]).astype(BF16)
            qd = _cat([q1 * qdec, q2 * qdec]).astype(BF16)
            kd = _cat([k1 * kdec, k2 * kdec]).astype(BF16)
            s = _dot_nt(qr, kr)
            state = state_ref[hd]
            cross = _dot(qd, state.astype(BF16))
            state_ref[hd] = (gammas[hd] ** SUPER) * state + _dot_tn(kd, vc)
            outs.append(_dot((s * dmat_ref[hd]).astype(BF16), vc) + cross)
        return outs[0] if len(outs) == 1 else jnp.concatenate(outs, axis=0)

    p = dict(pre_g=pre_g, b_in=b_in, conv_w=conv_w, conv_b=conv_b, gn_g=gn_g, post_g=post_g,
             w_in=w_in, w_co=w_co, w_ro=w_ro, w_o=w_o)
    for sub in range(rows // SUB_TILE):
        r = slice(sub * SUB_TILE, (sub + 1) * SUB_TILE)
        u_view = u_ref.at[sub * SUB_TILE:(sub + 1) * SUB_TILE + SUBLANES]
        out_ref[r, :] = _layer_rows(h_ref[r, :], cos_ref[r, :], sin_ref[r, :], p, u_view, retention)
    u_ref[0:SUBLANES, :] = u_ref[rows:rows + SUBLANES, :]


def _meta_kernel(meta_ref, cos_ref, sin_ref, dmat_ref, kdec_ref,
                 pre_g, b_in, conv_w, conv_b, gn_g, post_g, w_in, w_co, w_ro, w_o,
                 s_out_ref, utail_ref, hm_ref, u_ref):
    @pl.when(pl.program_id(0) == 0)
    def _():
        hm_ref[...] = meta_ref[...]
        u_ref[0:SUBLANES, :] = jnp.zeros((SUBLANES, D_MODEL), F32)

    def retention(hd, q, k, v):
        kdec = kdec_ref[hd]
        qr = _cat(q).astype(BF16)
        kr = _cat(k).astype(BF16)
        kd = _cat([k[0] * kdec, k[1] * kdec]).astype(BF16)
        s = (_dot_nt(qr, kr) * dmat_ref[hd]).astype(BF16)
        s_out_ref[hd] = _dot_tn(kd, v)
        return _dot(s, v)

    p = dict(pre_g=pre_g, b_in=b_in, conv_w=conv_w, conv_b=conv_b, gn_g=gn_g, post_g=post_g,
             w_in=w_in, w_co=w_co, w_ro=w_ro, w_o=w_o)
    hm_ref[...] = _layer_rows(hm_ref[...], cos_ref[...], sin_ref[...], p, u_ref, retention)
    utail_ref[...] = u_ref[N_META:N_META + SUBLANES, :]


def _tables(seq):
    total = N_META + seq
    pos = jnp.arange(total, dtype=F32)
    inv = ROPE_BASE ** (-jnp.arange(HALF, dtype=F32) / HALF)
    ang = pos[:, None] * inv[None, :]
    cos, sin = jnp.cos(ang), jnp.sin(ang)

    log_g = jnp.log(1.0 - 2.0 ** (-5.0 - jnp.arange(N_HEADS, dtype=F32)))[:, None, None]
    scale = DK ** -0.5
    i = jnp.arange(SUPER, dtype=F32)
    dist = jnp.abs(i[:, None] - i[None, :])
    visible = (jnp.arange(SUPER)[None, :] // CHUNK) <= (jnp.arange(SUPER)[:, None] // CHUNK)
    dmat = jnp.where(visible[None], jnp.exp(log_g * dist[None]), 0.0) * scale
    lane = jnp.ones((1, 1, HALF), F32)
    qdec = jnp.exp(log_g * (i[None, :, None] + 1.0)) * lane
    kdec = jnp.exp(log_g * (SUPER - 1.0 - i[None, :, None])) * scale * lane

    m = jnp.arange(META_ROWS, dtype=F32)
    is_meta = jnp.arange(META_ROWS) < N_META
    mdist = jnp.abs(m[:, None] - m[None, :])
    dmat_m = jnp.where((is_meta[:, None] & is_meta[None, :])[None], jnp.exp(log_g * mdist[None]), 0.0) * scale
    kdec_m = jnp.where(is_meta[None, :, None], jnp.exp(log_g * (N_META - 1.0 - m[None, :, None])), 0.0)
    kdec_m = kdec_m * scale * lane
    return dict(cos=cos, sin=sin, dmat=dmat, qdec=qdec, kdec=kdec, dmat_m=dmat_m, kdec_m=kdec_m)


def _resident(tail_shape, index_map):
    return pl.BlockSpec((None,) + tail_shape, index_map, pipeline_mode=pl.Buffered(1))


def _param_specs(params, layer_index_map):
    return [_resident(a.shape[1:], functools.partial(layer_index_map, nd=a.ndim - 1)) for a in params]


def _meta_layer_map(l, nd):
    return (l,) + (0,) * nd


def _main_layer_map(b, t, layer, nd):
    return (layer,) + (0,) * nd


def _const_map2(*_):
    return (0, 0)


def _const_map3(*_):
    return (0, 0, 0)


def _meta_call(meta_pad, tabs, params):
    in_specs = [
        pl.BlockSpec((META_ROWS, D_MODEL), _const_map2),
        pl.BlockSpec((META_ROWS, HALF), _const_map2),
        pl.BlockSpec((META_ROWS, HALF), _const_map2),
        pl.BlockSpec((N_HEADS, META_ROWS, META_ROWS), _const_map3),
        pl.BlockSpec((N_HEADS, META_ROWS, HALF), _const_map3),
    ] + _param_specs(params, _meta_layer_map)
    out_shape = (jax.ShapeDtypeStruct((DEPTH, N_HEADS, DK, DV), F32),
                 jax.ShapeDtypeStruct((DEPTH, SUBLANES, D_MODEL), F32))
    out_specs = (pl.BlockSpec((None, N_HEADS, DK, DV), lambda l: (l, 0, 0, 0)),
                 pl.BlockSpec((None, SUBLANES, D_MODEL), lambda l: (l, 0, 0)))
    return pl.pallas_call(
        _meta_kernel,
        grid=(DEPTH,),
        in_specs=in_specs,
        out_specs=out_specs,
        out_shape=out_shape,
        scratch_shapes=[pltpu.VMEM((META_ROWS, D_MODEL), F32),
                        pltpu.VMEM((SUBLANES + META_ROWS, D_MODEL), F32)],
        compiler_params=pltpu.CompilerParams(dimension_semantics=("arbitrary",),
                                             vmem_limit_bytes=VMEM_LIMIT_BYTES),
        name="meta_tokens",
    )(meta_pad, tabs["cos"][:META_ROWS], tabs["sin"][:META_ROWS], tabs["dmat_m"], tabs["kdec_m"], *params)


def _main_call(h, layer, tabs, s_init, utail, params):
    batch, seq, _ = h.shape
    layer_map = functools.partial(_main_layer_map, layer=layer)
    in_specs = [
        pl.BlockSpec((None, TOKEN_TILE, D_MODEL), lambda b, t: (b, t, 0)),
        pl.BlockSpec((TOKEN_TILE, HALF), lambda b, t: (t, 0)),
        pl.BlockSpec((TOKEN_TILE, HALF), lambda b, t: (t, 0)),
        pl.BlockSpec((N_HEADS, SUPER, SUPER), _const_map3, pipeline_mode=pl.Buffered(1)),
        pl.BlockSpec((N_HEADS, SUPER, HALF), _const_map3, pipeline_mode=pl.Buffered(1)),
        pl.BlockSpec((N_HEADS, SUPER, HALF), _const_map3, pipeline_mode=pl.Buffered(1)),
        _resident(s_init.shape[1:], functools.partial(layer_map, nd=3)),
        _resident(utail.shape[1:], functools.partial(layer_map, nd=2)),
    ] + _param_specs(params, layer_map)
    return pl.pallas_call(
        _main_kernel,
        grid=(batch, seq // TOKEN_TILE),
        in_specs=in_specs,
        out_specs=pl.BlockSpec((None, TOKEN_TILE, D_MODEL), lambda b, t: (b, t, 0)),
        out_shape=jax.ShapeDtypeStruct(h.shape, F32),
        scratch_shapes=[pltpu.VMEM((N_HEADS, DK, DV), F32),
                        pltpu.VMEM((SUBLANES + TOKEN_TILE, D_MODEL), F32)],
        compiler_params=pltpu.CompilerParams(dimension_semantics=("arbitrary", "arbitrary"),
                                             vmem_limit_bytes=VMEM_LIMIT_BYTES),
        name="mixer_layer",
    )(h, tabs["cos"][N_META:], tabs["sin"][N_META:], tabs["dmat"], tabs["qdec"], tabs["kdec"],
      s_init, utail, *params)


def kernel(x, meta, pre_norm_g, w_in, b_in, conv_w, conv_b, w_conv_out, ret_gn_g, w_ret_out, w_o, post_norm_g):
    batch, seq, d_model = x.shape
    assert d_model == D_MODEL and seq % TOKEN_TILE == 0 and meta.shape == (N_META, D_MODEL)
    assert w_in.shape == (DEPTH, D_MODEL, D_IN)
    tabs = _tables(seq)
    params = (
        pre_norm_g.reshape(DEPTH, 1, D_MODEL),
        b_in.reshape(DEPTH, 1, D_IN),
        conv_w,
        conv_b.reshape(DEPTH, 1, D_CONV),
        ret_gn_g.reshape(DEPTH, 1, D_V),
        post_norm_g.reshape(DEPTH, 1, D_MODEL),
        _pack_rows(w_in),
        _pack_rows(w_conv_out),
        _pack_rows(w_ret_out),
        _pack_rows(w_o),
    )
    meta_pad = jnp.zeros((META_ROWS, D_MODEL), F32).at[:N_META].set(meta.astype(F32))
    s_init, utail = _meta_call(meta_pad, tabs, params)
    h = x
    for layer in range(DEPTH):
        h = _main_call(h, layer, tabs, s_init, utail, params)
    return h
```

```python
import functools

import jax
import jax.numpy as jnp
from jax import lax
from jax.experimental import pallas as pl
from jax.experimental.pallas import tpu as pltpu

D_MODEL = 1024
DEPTH = 4
CHUNK = 64
N_META = 16
CONV_WIDTH = 3
D_CONV = D_MODEL
N_HEADS = 4
DK = 256
DV = 512
D_QK = N_HEADS * DK
D_V = N_HEADS * DV
ROPE_BASE = 10000.0
RMS_EPS = 1e-6
GN_EPS = 1e-5
D_IN = 4 * D_CONV + 2 * D_QK + 2 * D_V + 2 * D_MODEL

OFF_CH = 0
OFF_CB = OFF_CH + D_CONV
OFF_CC = OFF_CB + D_CONV
OFF_CZ = OFF_CC + D_CONV
OFF_Q = OFF_CZ + D_CONV
OFF_K = OFF_Q + D_QK
OFF_V = OFF_K + D_QK
OFF_RZ = OFF_V + D_V
OFF_GA = OFF_RZ + D_V
OFF_GB = OFF_GA + D_MODEL

HALF = DK // 2
SUPER = 4 * CHUNK
SUB_TILE = SUPER
TOKEN_TILE = 2 * SUB_TILE
META_ROWS = 128
SUBLANES = 8
VMEM_LIMIT_BYTES = 63 * 1024 * 1024
ROW_PACK = 2
PACKED = jnp.uint32
CAST_ROWS_WIDE = 16
CAST_ROWS = 128

F32 = jnp.float32
BF16 = jnp.bfloat16


def _gammas():
    return [1.0 - 2.0 ** (-5.0 - h) for h in range(N_HEADS)]


def _pack_rows(w):
    return pltpu.bitcast(w.astype(BF16), PACKED)


def _load_weight(src_hbm, layer, dst_ref, stage_ref, sem, chunk_rows):
    n_chunks = src_hbm.shape[1] // chunk_rows
    packed_rows = chunk_rows // ROW_PACK

    def chunk_copy(i, slot):
        src = src_hbm.at[layer, pl.ds(i * chunk_rows, chunk_rows), :]
        return pltpu.make_async_copy(src, stage_ref.at[slot], sem.at[slot])

    chunk_copy(0, 0).start()

    def body(i, carry):
        slot = i & 1

        @pl.when(i + 1 < n_chunks)
        def _():
            chunk_copy(i + 1, 1 - slot).start()

        chunk_copy(i, slot).wait()
        packed = _pack_rows(stage_ref[slot])
        dst_ref[pl.ds(pl.multiple_of(i * packed_rows, packed_rows), packed_rows), :] = packed
        return carry

    lax.fori_loop(0, n_chunks, body, 0)


def _weight(w_ref, rows, cols):
    packed = w_ref[rows.start // ROW_PACK:rows.stop // ROW_PACK, cols]
    return pltpu.bitcast(packed, BF16)


def _dot(a, b):
    return jnp.dot(a, b, preferred_element_type=F32)


def _dot_nt(a, b):
    return lax.dot_general(a, b, (((1,), (1,)), ((), ())), preferred_element_type=F32)


def _dot_tn(a, b):
    return lax.dot_general(a, b, (((0,), (0,)), ((), ())), preferred_element_type=F32)


def _rmsnorm(x, g):
    ms = jnp.mean(x * x, axis=-1, keepdims=True)
    return x * lax.rsqrt(ms + RMS_EPS) * g


def _silu(x):
    return x * jax.nn.sigmoid(x)


def _rope(x, cos, sin):
    x1, x2 = x[:, :HALF], x[:, HALF:]
    return x1 * cos - x2 * sin, x1 * sin + x2 * cos


def _layer_rows(h, cos, sin, p, u_ref, retention):
    rows = h.shape[0]
    xn = _rmsnorm(h, p["pre_g"][...]).astype(BF16)

    def proj(off, width):
        w = _weight(p["w_in"], slice(0, D_MODEL), slice(off, off + width))
        return _dot(xn, w) + p["b_in"][:, off:off + width]

    def head_proj(hd):
        return (proj(OFF_Q + hd * DK, DK), proj(OFF_K + hd * DK, DK),
                proj(OFF_V + hd * DV, DV), proj(OFF_RZ + hd * DV, DV))

    def head_rope(raw):
        q, k, v, z = raw
        return _rope(q, cos, sin), _rope(k, cos, sin), v.astype(BF16), z

    def head_norm(hd, o, z):
        mu = jnp.mean(o, axis=-1, keepdims=True)
        oc = o - mu
        var = jnp.mean(oc * oc, axis=-1, keepdims=True)
        on = oc * lax.rsqrt(var + GN_EPS) * p["gn_g"][:, hd * DV:(hd + 1) * DV]
        return (on * _silu(z)).astype(BF16)

    def head_out(hd, on):
        return _dot(on, _weight(p["w_ro"], slice(hd * DV, (hd + 1) * DV), slice(None)))

    raw = head_proj(0)
    y_b = None
    pending = None
    for hd in range(N_HEADS):
        q, k, v, z = head_rope(raw)
        on = None if pending is None else head_norm(*pending)
        if hd + 1 < N_HEADS:
            raw = head_proj(hd + 1)
        else:
            c_c, c_h = proj(OFF_CC, D_CONV), proj(OFF_CH, D_CONV)
        if on is not None:
            y_hd = head_out(pending[0], on)
            y_b = y_hd if y_b is None else y_b + y_hd
        pending = (hd, retention(hd, q, k, v), z)

    on = head_norm(*pending)
    u = c_c * c_h
    u_ref[SUBLANES:SUBLANES + rows, :] = u
    conv = p["conv_b"][...] + p["conv_w"][2:3, :] * u
    conv = conv + p["conv_w"][1:2, :] * u_ref[SUBLANES - 1:SUBLANES - 1 + rows, :]
    conv = conv + p["conv_w"][0:1, :] * u_ref[SUBLANES - 2:SUBLANES - 2 + rows, :]
    c_b, c_z = proj(OFF_CB, D_CONV), proj(OFF_CZ, D_CONV)
    y_b = y_b + head_out(pending[0], on)
    y = (c_b * conv * _silu(c_z)).astype(BF16)
    g_a, g_b = proj(OFF_GA, D_MODEL), proj(OFF_GB, D_MODEL)
    y_a = _dot(y, _weight(p["w_co"], slice(0, D_CONV), slice(None)))
    mix = jax.nn.sigmoid(g_a) * y_a + jax.nn.sigmoid(g_b) * y_b
    y_o = _dot(mix.astype(BF16), _weight(p["w_o"], slice(0, D_MODEL), slice(None)))
    return h + _rmsnorm(y_o, p["post_g"][...])


def _cat(halves):
    return jnp.concatenate(halves, axis=1)


def _meta_retention(dmat_ref, kdec_ref, s_out_ref):
    def retention(hd, q, k, v):
        kdec = kdec_ref[hd]
        qr = _cat(q).astype(BF16)
        kr = _cat(k).astype(BF16)
        kd = _cat([k[0] * kdec, k[1] * kdec]).astype(BF16)
        s = (_dot_nt(qr, kr) * dmat_ref[hd]).astype(BF16)
        s_out_ref[hd] = _dot_tn(kd, v)
        return _dot(s, v)
    return retention


def _main_kernel(h_ref, cos_ref, sin_ref, dmat_ref, qdec_ref, kdec_ref,
                 meta_ref, cosm_ref, sinm_ref, dmatm_ref, kdecm_ref,
                 pre_g, b_in, conv_w, conv_b, gn_g, post_g,
                 w_in_hbm, w_co_hbm, w_ro_hbm, w_o_hbm,
                 out_ref, meta_out_ref,
                 w_in, w_co, w_ro, w_o, stage_wide, stage, sem,
                 state_ref, u_ref, s_init_ref, utail_ref, um_ref, *, layer):
    rows = h_ref.shape[0]
    p = dict(pre_g=pre_g, b_in=b_in, conv_w=conv_w, conv_b=conv_b, gn_g=gn_g, post_g=post_g,
             w_in=w_in, w_co=w_co, w_ro=w_ro, w_o=w_o)

    @pl.when((pl.program_id(0) == 0) & (pl.program_id(1) == 0))
    def _():
        _load_weight(w_in_hbm, layer, w_in, stage_wide, sem, CAST_ROWS_WIDE)
        _load_weight(w_co_hbm, layer, w_co, stage, sem, CAST_ROWS)
        _load_weight(w_ro_hbm, layer, w_ro, stage, sem, CAST_ROWS)
        _load_weight(w_o_hbm, layer, w_o, stage, sem, CAST_ROWS)
        um_ref[0:SUBLANES, :] = jnp.zeros((SUBLANES, D_MODEL), F32)
        meta_out_ref[...] = _layer_rows(meta_ref[...], cosm_ref[...], sinm_ref[...], p, um_ref,
                                        _meta_retention(dmatm_ref, kdecm_ref, s_init_ref))
        utail_ref[...] = um_ref[N_META:N_META + SUBLANES, :]

    @pl.when(pl.program_id(1) == 0)
    def _():
        state_ref[...] = s_init_ref[...]
        u_ref[0:SUBLANES, :] = utail_ref[...]

    gammas = _gammas()

    def retention(hd, q, k, v):
        outs = []
        for c in range(v.shape[0] // SUPER):
            r = slice(c * SUPER, (c + 1) * SUPER)
            q1, q2 = q[0][r], q[1][r]
            k1, k2 = k[0][r], k[1][r]
            vc = v[r]
            qdec = qdec_ref[hd]
            kdec = kdec_ref[hd]
            qr = _cat([q1, q2]).astype(BF16)
            kr = _cat([k1, k2]).astype(BF16)
            qd = _cat([q1 * qdec, q2 * qdec]).astype(BF16)
            kd = _cat([k1 * kdec, k2 * kdec]).astype(BF16)
            s = _dot_nt(qr, kr)
            state = state_ref[hd]
            cross = _dot(qd, state.astype(BF16))
            state_ref[hd] = (gammas[hd] ** SUPER) * state + _dot_tn(kd, vc)
            outs.append(_dot((s * dmat_ref[hd]).astype(BF16), vc) + cross)
        return outs[0] if len(outs) == 1 else jnp.concatenate(outs, axis=0)

    for sub in range(rows // SUB_TILE):
        r = slice(sub * SUB_TILE, (sub + 1) * SUB_TILE)
        u_view = u_ref.at[sub * SUB_TILE:(sub + 1) * SUB_TILE + SUBLANES]
        out_ref[r, :] = _layer_rows(h_ref[r, :], cos_ref[r, :], sin_ref[r, :], p, u_view, retention)
    u_ref[0:SUBLANES, :] = u_ref[rows:rows + SUBLANES, :]


def _tables(seq):
    total = N_META + seq
    pos = jnp.arange(total, dtype=F32)
    inv = ROPE_BASE ** (-jnp.arange(HALF, dtype=F32) / HALF)
    ang = pos[:, None] * inv[None, :]
    cos, sin = jnp.cos(ang), jnp.sin(ang)

    log_g = jnp.log(1.0 - 2.0 ** (-5.0 - jnp.arange(N_HEADS, dtype=F32)))[:, None, None]
    scale = DK ** -0.5
    i = jnp.arange(SUPER, dtype=F32)
    dist = jnp.abs(i[:, None] - i[None, :])
    visible = (jnp.arange(SUPER)[None, :] // CHUNK) <= (jnp.arange(SUPER)[:, None] // CHUNK)
    dmat = jnp.where(visible[None], jnp.exp(log_g * dist[None]), 0.0) * scale
    lane = jnp.ones((1, 1, HALF), F32)
    qdec = jnp.exp(log_g * (i[None, :, None] + 1.0)) * lane
    kdec = jnp.exp(log_g * (SUPER - 1.0 - i[None, :, None])) * scale * lane

    m = jnp.arange(META_ROWS, dtype=F32)
    is_meta = jnp.arange(META_ROWS) < N_META
    mdist = jnp.abs(m[:, None] - m[None, :])
    dmat_m = jnp.where((is_meta[:, None] & is_meta[None, :])[None], jnp.exp(log_g * mdist[None]), 0.0) * scale
    kdec_m = jnp.where(is_meta[None, :, None], jnp.exp(log_g * (N_META - 1.0 - m[None, :, None])), 0.0)
    kdec_m = kdec_m * scale * lane
    return dict(cos=cos, sin=sin, dmat=dmat, qdec=qdec, kdec=kdec, dmat_m=dmat_m, kdec_m=kdec_m)


def _resident(block_shape, index_map):
    return pl.BlockSpec(block_shape, index_map, pipeline_mode=pl.Buffered(1))


def _layer_map(b, t, layer, nd):
    return (layer,) + (0,) * nd


def _const_map(b, t, nd):
    return (0,) * nd


def _main_call(h, meta_h, layer, tabs, small_params, weights):
    batch, seq, _ = h.shape
    consts = (meta_h, tabs["cos"][:META_ROWS], tabs["sin"][:META_ROWS], tabs["dmat_m"], tabs["kdec_m"])
    in_specs = [
        pl.BlockSpec((None, TOKEN_TILE, D_MODEL), lambda b, t: (b, t, 0)),
        pl.BlockSpec((TOKEN_TILE, HALF), lambda b, t: (t, 0)),
        pl.BlockSpec((TOKEN_TILE, HALF), lambda b, t: (t, 0)),
    ]
    in_specs += [_resident(a.shape, functools.partial(_const_map, nd=a.ndim))
                 for a in (tabs["dmat"], tabs["qdec"], tabs["kdec"]) + consts]
    in_specs += [_resident((None,) + a.shape[1:], functools.partial(_layer_map, layer=layer, nd=a.ndim - 1))
                 for a in small_params]
    in_specs += [pl.BlockSpec(memory_space=pl.ANY)] * len(weights)
    w_in, w_co, w_ro, w_o = weights
    return pl.pallas_call(
        functools.partial(_main_kernel, layer=layer),
        grid=(batch, seq // TOKEN_TILE),
        in_specs=in_specs,
        out_specs=(pl.BlockSpec((None, TOKEN_TILE, D_MODEL), lambda b, t: (b, t, 0)),
                   pl.BlockSpec((META_ROWS, D_MODEL), functools.partial(_const_map, nd=2))),
        out_shape=(jax.ShapeDtypeStruct(h.shape, F32), jax.ShapeDtypeStruct(meta_h.shape, F32)),
        scratch_shapes=[pltpu.VMEM((w_in.shape[1] // ROW_PACK, w_in.shape[2]), PACKED),
                        pltpu.VMEM((w_co.shape[1] // ROW_PACK, w_co.shape[2]), PACKED),
                        pltpu.VMEM((w_ro.shape[1] // ROW_PACK, w_ro.shape[2]), PACKED),
                        pltpu.VMEM((w_o.shape[1] // ROW_PACK, w_o.shape[2]), PACKED),
                        pltpu.VMEM((2, CAST_ROWS_WIDE, D_IN), F32),
                        pltpu.VMEM((2, CAST_ROWS, D_MODEL), F32),
                        pltpu.SemaphoreType.DMA((2,)),
                        pltpu.VMEM((N_HEADS, DK, DV), F32),
                        pltpu.VMEM((SUBLANES + TOKEN_TILE, D_MODEL), F32),
                        pltpu.VMEM((N_HEADS, DK, DV), F32),
                        pltpu.VMEM((SUBLANES, D_MODEL), F32),
                        pltpu.VMEM((SUBLANES + META_ROWS, D_MODEL), F32)],
        compiler_params=pltpu.CompilerParams(dimension_semantics=("arbitrary", "arbitrary"),
                                             vmem_limit_bytes=VMEM_LIMIT_BYTES),
        name="mixer_layer",
    )(h, tabs["cos"][N_META:], tabs["sin"][N_META:], tabs["dmat"], tabs["qdec"], tabs["kdec"],
      *consts, *small_params, *weights)


def kernel(x, meta, pre_norm_g, w_in, b_in, conv_w, conv_b, w_conv_out, ret_gn_g, w_ret_out, w_o, post_norm_g):
    batch, seq, d_model = x.shape
    assert d_model == D_MODEL and seq % TOKEN_TILE == 0 and meta.shape == (N_META, D_MODEL)
    assert w_in.shape == (DEPTH, D_MODEL, D_IN)
    tabs = _tables(seq)
    params = (
        pre_norm_g.reshape(DEPTH, 1, D_MODEL),
        b_in.reshape(DEPTH, 1, D_IN),
        conv_w,
        conv_b.reshape(DEPTH, 1, D_CONV),
        ret_gn_g.reshape(DEPTH, 1, D_V),
        post_norm_g.reshape(DEPTH, 1, D_MODEL),
    )
    weights = (w_in, w_conv_out, w_ret_out, w_o)
    meta_h = jnp.zeros((META_ROWS, D_MODEL), F32).at[:N_META].set(meta.astype(F32))
    h = x
    for layer in range(DEPTH):
        h, meta_h = _main_call(h, meta_h, layer, tabs, params, weights)
    return h
```

```python
import functools

import jax
import jax.numpy as jnp
from jax import lax
from jax.experimental import pallas as pl
from jax.experimental.pallas import tpu as pltpu

D_MODEL = 1024
DEPTH = 4
CHUNK = 64
N_META = 16
CONV_WIDTH = 3
D_CONV = D_MODEL
N_HEADS = 4
DK = 256
DV = 512
D_QK = N_HEADS * DK
D_V = N_HEADS * DV
ROPE_BASE = 10000.0
RMS_EPS = 1e-6
GN_EPS = 1e-5
D_IN = 4 * D_CONV + 2 * D_QK + 2 * D_V + 2 * D_MODEL

OFF_CH = 0
OFF_CB = OFF_CH + D_CONV
OFF_CC = OFF_CB + D_CONV
OFF_CZ = OFF_CC + D_CONV
OFF_Q = OFF_CZ + D_CONV
OFF_K = OFF_Q + D_QK
OFF_V = OFF_K + D_QK
OFF_RZ = OFF_V + D_V
OFF_GA = OFF_RZ + D_V
OFF_GB = OFF_GA + D_MODEL

HALF = DK // 2
SUPER = 4 * CHUNK
SUB_TILE = SUPER
TOKEN_TILE = 2 * SUB_TILE
META_ROWS = 128
SUBLANES = 8
VMEM_LIMIT_BYTES = 60 * 1024 * 1024
ROW_PACK = 2
PACK_BLOCK_ROWS = 512
PACK_BLOCK_COLS = 2048

F32 = jnp.float32
BF16 = jnp.bfloat16


def _gammas():
    return [1.0 - 2.0 ** (-5.0 - h) for h in range(N_HEADS)]


def _pack_kernel(w_ref, o_ref):
    o_ref[...] = pltpu.bitcast(w_ref[...].astype(BF16), jnp.uint32)


def _pack_rows(w):
    layers, k, n = w.shape
    bk, bn = min(k, PACK_BLOCK_ROWS), min(n, PACK_BLOCK_COLS)
    assert k % bk == 0 and n % bn == 0
    return pl.pallas_call(
        _pack_kernel,
        grid=(layers, k // bk, n // bn),
        in_specs=[pl.BlockSpec((None, bk, bn), lambda l, i, j: (l, i, j))],
        out_specs=pl.BlockSpec((None, bk // ROW_PACK, bn), lambda l, i, j: (l, i, j)),
        out_shape=jax.ShapeDtypeStruct((layers, k // ROW_PACK, n), jnp.uint32),
        compiler_params=pltpu.CompilerParams(dimension_semantics=("arbitrary", "arbitrary", "arbitrary")),
        name="pack_weight_rows",
    )(w)


def _weight(w_ref, rows, cols):
    packed = w_ref[rows.start // ROW_PACK:rows.stop // ROW_PACK, cols]
    return pltpu.bitcast(packed, BF16)


def _dot(a, b):
    return jnp.dot(a, b, preferred_element_type=F32)


def _dot_nt(a, b):
    return lax.dot_general(a, b, (((1,), (1,)), ((), ())), preferred_element_type=F32)


def _dot_tn(a, b):
    return lax.dot_general(a, b, (((0,), (0,)), ((), ())), preferred_element_type=F32)


def _rmsnorm(x, g):
    ms = jnp.mean(x * x, axis=-1, keepdims=True)
    return x * lax.rsqrt(ms + RMS_EPS) * g


def _silu(x):
    return x * jax.nn.sigmoid(x)


def _rope(x, cos, sin):
    x1, x2 = x[:, :HALF], x[:, HALF:]
    return x1 * cos - x2 * sin, x1 * sin + x2 * cos


def _layer_rows(h, cos, sin, p, u_ref, retention):
    rows = h.shape[0]
    xn = _rmsnorm(h, p["pre_g"][...]).astype(BF16)

    def proj(off, width):
        w = _weight(p["w_in"], slice(0, D_MODEL), slice(off, off + width))
        return _dot(xn, w) + p["b_in"][:, off:off + width]

    def head_proj(hd):
        return (proj(OFF_Q + hd * DK, DK), proj(OFF_K + hd * DK, DK),
                proj(OFF_V + hd * DV, DV), proj(OFF_RZ + hd * DV, DV))

    def head_rope(raw):
        q, k, v, z = raw
        return _rope(q, cos, sin), _rope(k, cos, sin), v.astype(BF16), z.astype(BF16)

    def head_norm(hd, o, z):
        mu = jnp.mean(o, axis=-1, keepdims=True)
        oc = o - mu
        var = jnp.mean(oc * oc, axis=-1, keepdims=True)
        on = oc * lax.rsqrt(var + GN_EPS) * p["gn_g"][:, hd * DV:(hd + 1) * DV]
        return on.astype(BF16) * _silu(z)

    def head_out(hd, on):
        return _dot(on, _weight(p["w_ro"], slice(hd * DV, (hd + 1) * DV), slice(None)))

    raw = head_proj(0)
    y_b = None
    pending = None
    for hd in range(N_HEADS):
        q, k, v, z = head_rope(raw)
        on = None if pending is None else head_norm(*pending)
        if hd + 1 < N_HEADS:
            raw = head_proj(hd + 1)
        else:
            c_c, c_h = proj(OFF_CC, D_CONV), proj(OFF_CH, D_CONV)
        if on is not None:
            y_hd = head_out(pending[0], on)
            y_b = y_hd if y_b is None else y_b + y_hd
        pending = (hd, retention(hd, q, k, v), z)

    on = head_norm(*pending)
    u = c_c * c_h
    u_ref[SUBLANES:SUBLANES + rows, :] = u
    conv = p["conv_b"][...] + p["conv_w"][2:3, :] * u
    conv = conv + p["conv_w"][1:2, :] * u_ref[SUBLANES - 1:SUBLANES - 1 + rows, :]
    conv = conv + p["conv_w"][0:1, :] * u_ref[SUBLANES - 2:SUBLANES - 2 + rows, :]
    c_b, c_z = proj(OFF_CB, D_CONV), proj(OFF_CZ, D_CONV)
    y_b = y_b + head_out(pending[0], on)
    y = (c_b * conv).astype(BF16) * _silu(c_z.astype(BF16))
    g_a, g_b = proj(OFF_GA, D_MODEL), proj(OFF_GB, D_MODEL)
    y_a = _dot(y, _weight(p["w_co"], slice(0, D_CONV), slice(None)))
    mix = jax.nn.sigmoid(g_a) * y_a + jax.nn.sigmoid(g_b) * y_b
    y_o = _dot(mix.astype(BF16), _weight(p["w_o"], slice(0, D_MODEL), slice(None)))
    return h + _rmsnorm(y_o, p["post_g"][...])


def _cat(halves):
    return jnp.concatenate(halves, axis=1)


def _main_kernel(h_ref, cos_ref, sin_ref, dmat_ref, qdec_ref, kdec_ref, s_init_ref, utail_ref,
                 pre_g, b_in, conv_w, conv_b, gn_g, post_g, w_in, w_co, w_ro, w_o,
                 out_ref, state_ref, u_ref):
    rows = h_ref.shape[0]

    @pl.when(pl.program_id(1) == 0)
    def _():
        state_ref[...] = s_init_ref[...]
        u_ref[0:SUBLANES, :] = utail_ref[...]

    gammas = _gammas()

    def retention(hd, q, k, v):
        outs = []
        for c in range(v.shape[0] // SUPER):
            r = slice(c * SUPER, (c + 1) * SUPER)
            q1, q2 = q[0][r], q[1][r]
            k1, k2 = k[0][r], k[1][r]
            vc = v[r]
            qdec = qdec_ref[hd]
            kdec = kdec_ref[hd]
            qr = _cat([q1, q2]).astype(BF16)
            kr = _cat([k1, k2]).astype(BF16)
            qd = _cat([q1 * qdec, q2 * qdec]).astype(BF16)
            kd = _cat([k1 * kdec, k2 * kdec]).astype(BF16)
            s = _dot_nt(qr, kr)
            state = state_ref[hd]
            cross = _dot(qd, state.astype(BF16))
            state_ref[hd] = (gammas[hd] ** SUPER) * state + _dot_tn(kd, vc)
            outs.append(_dot((s * dmat_ref[hd]).astype(BF16), vc) + cross)
        return outs[0] if len(outs) == 1 else jnp.concatenate(outs, axis=0)

    p = dict(pre_g=pre_g, b_in=b_in, conv_w=conv_w, conv_b=conv_b, gn_g=gn_g, post_g=post_g,
             w_in=w_in, w_co=w_co, w_ro=w_ro, w_o=w_o)
    for sub in range(rows // SUB_TILE):
        r = slice(sub * SUB_TILE, (sub + 1) * SUB_TILE)
        u_view = u_ref.at[sub * SUB_TILE:(sub + 1) * SUB_TILE + SUBLANES]
        out_ref[r, :] = _layer_rows(h_ref[r, :], cos_ref[r, :], sin_ref[r, :], p, u_view, retention)
    u_ref[0:SUBLANES, :] = u_ref[rows:rows + SUBLANES, :]


def _meta_kernel(meta_ref, cos_ref, sin_ref, dmat_ref, kdec_ref,
                 pre_g, b_in, conv_w, conv_b, gn_g, post_g, w_in, w_co, w_ro, w_o,
                 s_out_ref, utail_ref, hm_ref, u_ref):
    @pl.when(pl.program_id(0) == 0)
    def _():
        hm_ref[...] = meta_ref[...]
        u_ref[0:SUBLANES, :] = jnp.zeros((SUBLANES, D_MODEL), F32)

    def retention(hd, q, k, v):
        kdec = kdec_ref[hd]
        qr = _cat(q).astype(BF16)
        kr = _cat(k).astype(BF16)
        kd = _cat([k[0] * kdec, k[1] * kdec]).astype(BF16)
        s = (_dot_nt(qr, kr) * dmat_ref[hd]).astype(BF16)
        s_out_ref[hd] = _dot_tn(kd, v)
        return _dot(s, v)

    p = dict(pre_g=pre_g, b_in=b_in, conv_w=conv_w, conv_b=conv_b, gn_g=gn_g, post_g=post_g,
             w_in=w_in, w_co=w_co, w_ro=w_ro, w_o=w_o)
    hm_ref[...] = _layer_rows(hm_ref[...], cos_ref[...], sin_ref[...], p, u_ref, retention)
    utail_ref[...] = u_ref[N_META:N_META + SUBLANES, :]


def _tables(seq):
    total = N_META + seq
    pos = jnp.arange(total, dtype=F32)
    inv = ROPE_BASE ** (-jnp.arange(HALF, dtype=F32) / HALF)
    ang = pos[:, None] * inv[None, :]
    cos, sin = jnp.cos(ang), jnp.sin(ang)

    log_g = jnp.log(1.0 - 2.0 ** (-5.0 - jnp.arange(N_HEADS, dtype=F32)))[:, None, None]
    scale = DK ** -0.5
    i = jnp.arange(SUPER, dtype=F32)
    dist = jnp.abs(i[:, None] - i[None, :])
    visible = (jnp.arange(SUPER)[None, :] // CHUNK) <= (jnp.arange(SUPER)[:, None] // CHUNK)
    dmat = jnp.where(visible[None], jnp.exp(log_g * dist[None]), 0.0) * scale
    lane = jnp.ones((1, 1, HALF), F32)
    qdec = jnp.exp(log_g * (i[None, :, None] + 1.0)) * lane
    kdec = jnp.exp(log_g * (SUPER - 1.0 - i[None, :, None])) * scale * lane

    m = jnp.arange(META_ROWS, dtype=F32)
    is_meta = jnp.arange(META_ROWS) < N_META
    mdist = jnp.abs(m[:, None] - m[None, :])
    dmat_m = jnp.where((is_meta[:, None] & is_meta[None, :])[None], jnp.exp(log_g * mdist[None]), 0.0) * scale
    kdec_m = jnp.where(is_meta[None, :, None], jnp.exp(log_g * (N_META - 1.0 - m[None, :, None])), 0.0)
    kdec_m = kdec_m * scale * lane
    return dict(cos=cos, sin=sin, dmat=dmat, qdec=qdec, kdec=kdec, dmat_m=dmat_m, kdec_m=kdec_m)


def _resident(tail_shape, index_map):
    return pl.BlockSpec((None,) + tail_shape, index_map, pipeline_mode=pl.Buffered(1))


def _param_specs(params, layer_index_map):
    return [_resident(a.shape[1:], functools.partial(layer_index_map, nd=a.ndim - 1)) for a in params]


def _meta_layer_map(l, nd):
    return (l,) + (0,) * nd


def _main_layer_map(b, t, layer, nd):
    return (layer,) + (0,) * nd


def _const_map2(*_):
    return (0, 0)


def _const_map3(*_):
    return (0, 0, 0)


def _meta_call(meta_pad, tabs, params):
    in_specs = [
        pl.BlockSpec((META_ROWS, D_MODEL), _const_map2),
        pl.BlockSpec((META_ROWS, HALF), _const_map2),
        pl.BlockSpec((META_ROWS, HALF), _const_map2),
        pl.BlockSpec((N_HEADS, META_ROWS, META_ROWS), _const_map3),
        pl.BlockSpec((N_HEADS, META_ROWS, HALF), _const_map3),
    ] + _param_specs(params, _meta_layer_map)
    out_shape = (jax.ShapeDtypeStruct((DEPTH, N_HEADS, DK, DV), F32),
                 jax.ShapeDtypeStruct((DEPTH, SUBLANES, D_MODEL), F32))
    out_specs = (pl.BlockSpec((None, N_HEADS, DK, DV), lambda l: (l, 0, 0, 0)),
                 pl.BlockSpec((None, SUBLANES, D_MODEL), lambda l: (l, 0, 0)))
    return pl.pallas_call(
        _meta_kernel,
        grid=(DEPTH,),
        in_specs=in_specs,
        out_specs=out_specs,
        out_shape=out_shape,
        scratch_shapes=[pltpu.VMEM((META_ROWS, D_MODEL), F32),
                        pltpu.VMEM((SUBLANES + META_ROWS, D_MODEL), F32)],
        compiler_params=pltpu.CompilerParams(dimension_semantics=("arbitrary",),
                                             vmem_limit_bytes=VMEM_LIMIT_BYTES),
        name="meta_tokens",
    )(meta_pad, tabs["cos"][:META_ROWS], tabs["sin"][:META_ROWS], tabs["dmat_m"], tabs["kdec_m"], *params)


def _main_call(h, layer, tabs, s_init, utail, params):
    batch, seq, _ = h.shape
    layer_map = functools.partial(_main_layer_map, layer=layer)
    in_specs = [
        pl.BlockSpec((None, TOKEN_TILE, D_MODEL), lambda b, t: (b, t, 0)),
        pl.BlockSpec((TOKEN_TILE, HALF), lambda b, t: (t, 0)),
        pl.BlockSpec((TOKEN_TILE, HALF), lambda b, t: (t, 0)),
        pl.BlockSpec((N_HEADS, SUPER, SUPER), _const_map3, pipeline_mode=pl.Buffered(1)),
        pl.BlockSpec((N_HEADS, SUPER, HALF), _const_map3, pipeline_mode=pl.Buffered(1)),
        pl.BlockSpec((N_HEADS, SUPER, HALF), _const_map3, pipeline_mode=pl.Buffered(1)),
        _resident(s_init.shape[1:], functools.partial(layer_map, nd=3)),
        _resident(utail.shape[1:], functools.partial(layer_map, nd=2)),
    ] + _param_specs(params, layer_map)
    return pl.pallas_call(
        _main_kernel,
        grid=(batch, seq // TOKEN_TILE),
        in_specs=in_specs,
        out_specs=pl.BlockSpec((None, TOKEN_TILE, D_MODEL), lambda b, t: (b, t, 0)),
        out_shape=jax.ShapeDtypeStruct(h.shape, F32),
        scratch_shapes=[pltpu.VMEM((N_HEADS, DK, DV), F32),
                        pltpu.VMEM((SUBLANES + TOKEN_TILE, D_MODEL), F32)],
        compiler_params=pltpu.CompilerParams(dimension_semantics=("arbitrary", "arbitrary"),
                                             vmem_limit_bytes=VMEM_LIMIT_BYTES),
        name="mixer_layer",
    )(h, tabs["cos"][N_META:], tabs["sin"][N_META:], tabs["dmat"], tabs["qdec"], tabs["kdec"],
      s_init, utail, *params)


def kernel(x, meta, pre_norm_g, w_in, b_in, conv_w, conv_b, w_conv_out, ret_gn_g, w_ret_out, w_o, post_norm_g):
    batch, seq, d_model = x.shape
    assert d_model == D_MODEL and seq % TOKEN_TILE == 0 and meta.shape == (N_META, D_MODEL)
    assert w_in.shape == (DEPTH, D_MODEL, D_IN)
    tabs = _tables(seq)
    params = (
        pre_norm_g.reshape(DEPTH, 1, D_MODEL),
        b_in.reshape(DEPTH, 1, D_IN),
        conv_w,
        conv_b.reshape(DEPTH, 1, D_CONV),
        ret_gn_g.reshape(DEPTH, 1, D_V),
        post_norm_g.reshape(DEPTH, 1, D_MODEL),
        _pack_rows(w_in),
        _pack_rows(w_conv_out),
        _pack_rows(w_ret_out),
        _pack_rows(w_o),
    )
    meta_pad = jnp.zeros((META_ROWS, D_MODEL), F32).at[:N_META].set(meta.astype(F32))
    s_init, utail = _meta_call(meta_pad, tabs, params)
    h = x
    for layer in range(DEPTH):
        h = _main_call(h, layer, tabs, s_init, utail, params)
    return h
```

```python
import functools

import jax
import jax.numpy as jnp
from jax import lax
from jax.experimental import pallas as pl
from jax.experimental.pallas import tpu as pltpu

D_MODEL = 1024
DEPTH = 4
CHUNK = 64
N_META = 16
CONV_WIDTH = 3
D_CONV = D_MODEL
N_HEADS = 4
DK = 256
DV = 512
D_QK = N_HEADS * DK
D_V = N_HEADS * DV
ROPE_BASE = 10000.0
RMS_EPS = 1e-6
GN_EPS = 1e-5
D_IN = 4 * D_CONV + 2 * D_QK + 2 * D_V + 2 * D_MODEL

OFF_CH = 0
OFF_CB = OFF_CH + D_CONV
OFF_CC = OFF_CB + D_CONV
OFF_CZ = OFF_CC + D_CONV
OFF_Q = OFF_CZ + D_CONV
OFF_K = OFF_Q + D_QK
OFF_V = OFF_K + D_QK
OFF_RZ = OFF_V + D_V
OFF_GA = OFF_RZ + D_V
OFF_GB = OFF_GA + D_MODEL

HALF = DK // 2
SUPER = 4 * CHUNK
SUB_TILE = SUPER
TOKEN_TILE = 2 * SUB_TILE
META_ROWS = 128
SUBLANES = 8
VMEM_LIMIT_BYTES = 62 * 1024 * 1024
ROW_PACK = 2
PACKED = jnp.uint32
CAST_ROWS = DK
CAST_COLS = DV

F32 = jnp.float32
BF16 = jnp.bfloat16


def _gammas():
    return [1.0 - 2.0 ** (-5.0 - h) for h in range(N_HEADS)]


def _pack_rows(w):
    return pltpu.bitcast(w.astype(BF16), PACKED)


def _load_weights(layer, weights, stages, sem):
    chunks = [(src, dst, r0, c0) for src, dst in weights
              for r0 in range(0, src.shape[1], CAST_ROWS) for c0 in range(0, src.shape[2], CAST_COLS)]
    depth = len(stages)

    def chunk_copy(j):
        src, _, r0, c0 = chunks[j]
        view = src.at[layer, r0:r0 + CAST_ROWS, c0:c0 + CAST_COLS]
        return pltpu.make_async_copy(view, stages[j % depth], sem.at[j % depth])

    for j in range(min(depth - 1, len(chunks))):
        chunk_copy(j).start()
    for j, (_, dst, r0, c0) in enumerate(chunks):
        chunk_copy(j).wait()
        dst[r0 // ROW_PACK:(r0 + CAST_ROWS) // ROW_PACK, c0:c0 + CAST_COLS] = _pack_rows(stages[j % depth][...])
        if j + depth - 1 < len(chunks):
            chunk_copy(j + depth - 1).start()


def _weight(w_ref, rows, cols):
    packed = w_ref[rows.start // ROW_PACK:rows.stop // ROW_PACK, cols]
    return pltpu.bitcast(packed, BF16)


def _dot(a, b):
    return jnp.dot(a, b, preferred_element_type=F32)


def _dot_nt(a, b):
    return lax.dot_general(a, b, (((1,), (1,)), ((), ())), preferred_element_type=F32)


def _dot_tn(a, b):
    return lax.dot_general(a, b, (((0,), (0,)), ((), ())), preferred_element_type=F32)


def _rmsnorm(x, g):
    ms = jnp.mean(x * x, axis=-1, keepdims=True)
    return x * lax.rsqrt(ms + RMS_EPS) * g


def _silu(x):
    return x * jax.nn.sigmoid(x)


def _rope(x, cos, sin):
    x1, x2 = x[:, :HALF], x[:, HALF:]
    return x1 * cos - x2 * sin, x1 * sin + x2 * cos


def _layer_rows(h, cos, sin, p, u_ref, retention):
    rows = h.shape[0]
    xn = _rmsnorm(h, p["pre_g"][...]).astype(BF16)

    def proj(off, width):
        w = _weight(p["w_in"], slice(0, D_MODEL), slice(off, off + width))
        return _dot(xn, w) + p["b_in"][:, off:off + width]

    def head_proj(hd):
        return (proj(OFF_Q + hd * DK, DK), proj(OFF_K + hd * DK, DK),
                proj(OFF_V + hd * DV, DV), proj(OFF_RZ + hd * DV, DV))

    def head_rope(raw):
        q, k, v, z = raw
        return _rope(q, cos, sin), _rope(k, cos, sin), v.astype(BF16), z

    def head_norm(hd, o, z):
        mu = jnp.mean(o, axis=-1, keepdims=True)
        oc = o - mu
        var = jnp.mean(oc * oc, axis=-1, keepdims=True)
        on = oc * lax.rsqrt(var + GN_EPS) * p["gn_g"][:, hd * DV:(hd + 1) * DV]
        return (on * _silu(z)).astype(BF16)

    def head_out(hd, on):
        return _dot(on, _weight(p["w_ro"], slice(hd * DV, (hd + 1) * DV), slice(None)))

    raw = head_proj(0)
    y_b = None
    pending = None
    for hd in range(N_HEADS):
        q, k, v, z = head_rope(raw)
        on = None if pending is None else head_norm(*pending)
        if hd + 1 < N_HEADS:
            raw = head_proj(hd + 1)
        else:
            c_c, c_h = proj(OFF_CC, D_CONV), proj(OFF_CH, D_CONV)
        if on is not None:
            y_hd = head_out(pending[0], on)
            y_b = y_hd if y_b is None else y_b + y_hd
        pending = (hd, retention(hd, q, k, v), z)

    on = head_norm(*pending)
    u = c_c * c_h
    u_ref[SUBLANES:SUBLANES + rows, :] = u
    conv = p["conv_b"][...] + p["conv_w"][2:3, :] * u
    conv = conv + p["conv_w"][1:2, :] * u_ref[SUBLANES - 1:SUBLANES - 1 + rows, :]
    conv = conv + p["conv_w"][0:1, :] * u_ref[SUBLANES - 2:SUBLANES - 2 + rows, :]
    c_b, c_z = proj(OFF_CB, D_CONV), proj(OFF_CZ, D_CONV)
    y_b = y_b + head_out(pending[0], on)
    y = (c_b * conv * _silu(c_z)).astype(BF16)
    g_a, g_b = proj(OFF_GA, D_MODEL), proj(OFF_GB, D_MODEL)
    y_a = _dot(y, _weight(p["w_co"], slice(0, D_CONV), slice(None)))
    mix = jax.nn.sigmoid(g_a) * y_a + jax.nn.sigmoid(g_b) * y_b
    y_o = _dot(mix.astype(BF16), _weight(p["w_o"], slice(0, D_MODEL), slice(None)))
    return h + _rmsnorm(y_o, p["post_g"][...])


def _cat(halves):
    return jnp.concatenate(halves, axis=1)


def _meta_retention(dmat_ref, kdec_ref, s_out_ref):
    def retention(hd, q, k, v):
        kdec = kdec_ref[hd]
        qr = _cat(q).astype(BF16)
        kr = _cat(k).astype(BF16)
        kd = _cat([k[0] * kdec, k[1] * kdec]).astype(BF16)
        s = (_dot_nt(qr, kr) * dmat_ref[hd]).astype(BF16)
        s_out_ref[hd] = _dot_tn(kd, v)
        return _dot(s, v)
    return retention


def _main_kernel(h_ref, cos_ref, sin_ref, dmat_ref, qdec_ref, kdec_ref,
                 meta_ref, cosm_ref, sinm_ref, dmatm_ref, kdecm_ref,
                 pre_g, b_in, conv_w, conv_b, gn_g, post_g,
                 w_in_hbm, w_co_hbm, w_ro_hbm, w_o_hbm,
                 out_ref, meta_out_ref,
                 w_in, w_co, w_ro, w_o, sem,
                 state_ref, u_ref, s_init_ref, utail_ref, um_ref, *, layer):
    rows = h_ref.shape[0]
    p = dict(pre_g=pre_g, b_in=b_in, conv_w=conv_w, conv_b=conv_b, gn_g=gn_g, post_g=post_g,
             w_in=w_in, w_co=w_co, w_ro=w_ro, w_o=w_o)

    @pl.when((pl.program_id(0) == 0) & (pl.program_id(1) == 0))
    def _():
        stages = [ref.at[hd] for ref in (state_ref, s_init_ref) for hd in range(N_HEADS)]
        _load_weights(layer, ((w_in_hbm, w_in), (w_co_hbm, w_co), (w_ro_hbm, w_ro), (w_o_hbm, w_o)),
                      stages, sem)
        um_ref[0:SUBLANES, :] = jnp.zeros((SUBLANES, D_MODEL), F32)
        meta_out_ref[...] = _layer_rows(meta_ref[...], cosm_ref[...], sinm_ref[...], p, um_ref,
                                        _meta_retention(dmatm_ref, kdecm_ref, s_init_ref))
        utail_ref[...] = um_ref[N_META:N_META + SUBLANES, :]

    @pl.when(pl.program_id(1) == 0)
    def _():
        state_ref[...] = s_init_ref[...]
        u_ref[0:SUBLANES, :] = utail_ref[...]

    gammas = _gammas()

    def retention(hd, q, k, v):
        outs = []
        for c in range(v.shape[0] // SUPER):
            r = slice(c * SUPER, (c + 1) * SUPER)
            q1, q2 = q[0][r], q[1][r]
            k1, k2 = k[0][r], k[1][r]
            vc = v[r]
            qdec = qdec_ref[hd]
            kdec = kdec_ref[hd]
            qr = _cat([q1, q2]).astype(BF16)
            kr = _cat([k1, k2]).astype(BF16)
            qd = _cat([q1 * qdec, q2 * qdec]).astype(BF16)
            kd = _cat([k1 * kdec, k2 * kdec]).astype(BF16)
            s = _dot_nt(qr, kr)
            state = state_ref[hd]
            cross = _dot(qd, state.astype(BF16))
            state_ref[hd] = (gammas[hd] ** SUPER) * state + _dot_tn(kd, vc)
            outs.append(_dot((s * dmat_ref[hd]).astype(BF16), vc) + cross)
        return outs[0] if len(outs) == 1 else jnp.concatenate(outs, axis=0)

    for sub in range(rows // SUB_TILE):
        r = slice(sub * SUB_TILE, (sub + 1) * SUB_TILE)
        u_view = u_ref.at[sub * SUB_TILE:(sub + 1) * SUB_TILE + SUBLANES]
        out_ref[r, :] = _layer_rows(h_ref[r, :], cos_ref[r, :], sin_ref[r, :], p, u_view, retention)
    u_ref[0:SUBLANES, :] = u_ref[rows:rows + SUBLANES, :]


def _tables(seq):
    total = N_META + seq
    pos = jnp.arange(total, dtype=F32)
    inv = ROPE_BASE ** (-jnp.arange(HALF, dtype=F32) / HALF)
    ang = pos[:, None] * inv[None, :]
    cos, sin = jnp.cos(ang), jnp.sin(ang)

    log_g = jnp.log(1.0 - 2.0 ** (-5.0 - jnp.arange(N_HEADS, dtype=F32)))[:, None, None]
    scale = DK ** -0.5
    i = jnp.arange(SUPER, dtype=F32)
    dist = jnp.abs(i[:, None] - i[None, :])
    visible = (jnp.arange(SUPER)[None, :] // CHUNK) <= (jnp.arange(SUPER)[:, None] // CHUNK)
    dmat = jnp.where(visible[None], jnp.exp(log_g * dist[None]), 0.0) * scale
    lane = jnp.ones((1, 1, HALF), F32)
    qdec = jnp.exp(log_g * (i[None, :, None] + 1.0)) * lane
    kdec = jnp.exp(log_g * (SUPER - 1.0 - i[None, :, None])) * scale * lane

    m = jnp.arange(META_ROWS, dtype=F32)
    is_meta = jnp.arange(META_ROWS) < N_META
    mdist = jnp.abs(m[:, None] - m[None, :])
    dmat_m = jnp.where((is_meta[:, None] & is_meta[None, :])[None], jnp.exp(log_g * mdist[None]), 0.0) * scale
    kdec_m = jnp.where(is_meta[None, :, None], jnp.exp(log_g * (N_META - 1.0 - m[None, :, None])), 0.0)
    kdec_m = kdec_m * scale * lane
    return dict(cos=cos, sin=sin, dmat=dmat, qdec=qdec, kdec=kdec, dmat_m=dmat_m, kdec_m=kdec_m)


def _resident(block_shape, index_map):
    return pl.BlockSpec(block_shape, index_map, pipeline_mode=pl.Buffered(1))


def _layer_map(b, t, layer, nd):
    return (layer,) + (0,) * nd


def _const_map(b, t, nd):
    return (0,) * nd


def _main_call(h, meta_h, layer, tabs, small_params, weights):
    batch, seq, _ = h.shape
    consts = (meta_h, tabs["cos"][:META_ROWS], tabs["sin"][:META_ROWS], tabs["dmat_m"], tabs["kdec_m"])
    in_specs = [
        pl.BlockSpec((None, TOKEN_TILE, D_MODEL), lambda b, t: (b, t, 0)),
        pl.BlockSpec((TOKEN_TILE, HALF), lambda b, t: (t, 0)),
        pl.BlockSpec((TOKEN_TILE, HALF), lambda b, t: (t, 0)),
    ]
    in_specs += [_resident(a.shape, functools.partial(_const_map, nd=a.ndim))
                 for a in (tabs["dmat"], tabs["qdec"], tabs["kdec"]) + consts]
    in_specs += [_resident((None,) + a.shape[1:], functools.partial(_layer_map, layer=layer, nd=a.ndim - 1))
                 for a in small_params]
    in_specs += [pl.BlockSpec(memory_space=pl.ANY)] * len(weights)
    w_in, w_co, w_ro, w_o = weights
    return pl.pallas_call(
        functools.partial(_main_kernel, layer=layer),
        grid=(batch, seq // TOKEN_TILE),
        in_specs=in_specs,
        out_specs=(pl.BlockSpec((None, TOKEN_TILE, D_MODEL), lambda b, t: (b, t, 0)),
                   pl.BlockSpec((META_ROWS, D_MODEL), functools.partial(_const_map, nd=2))),
        out_shape=(jax.ShapeDtypeStruct(h.shape, F32), jax.ShapeDtypeStruct(meta_h.shape, F32)),
        scratch_shapes=[pltpu.VMEM((w_in.shape[1] // ROW_PACK, w_in.shape[2]), PACKED),
                        pltpu.VMEM((w_co.shape[1] // ROW_PACK, w_co.shape[2]), PACKED),
                        pltpu.VMEM((w_ro.shape[1] // ROW_PACK, w_ro.shape[2]), PACKED),
                        pltpu.VMEM((w_o.shape[1] // ROW_PACK, w_o.shape[2]), PACKED),
                        pltpu.SemaphoreType.DMA((2 * N_HEADS,)),
                        pltpu.VMEM((N_HEADS, DK, DV), F32),
                        pltpu.VMEM((SUBLANES + TOKEN_TILE, D_MODEL), F32),
                        pltpu.VMEM((N_HEADS, DK, DV), F32),
                        pltpu.VMEM((SUBLANES, D_MODEL), F32),
                        pltpu.VMEM((SUBLANES + META_ROWS, D_MODEL), F32)],
        compiler_params=pltpu.CompilerParams(dimension_semantics=("arbitrary", "arbitrary"),
                                             vmem_limit_bytes=VMEM_LIMIT_BYTES),
        name="mixer_layer",
    )(h, tabs["cos"][N_META:], tabs["sin"][N_META:], tabs["dmat"], tabs["qdec"], tabs["kdec"],
      *consts, *small_params, *weights)


def kernel(x, meta, pre_norm_g, w_in, b_in, conv_w, conv_b, w_conv_out, ret_gn_g, w_ret_out, w_o, post_norm_g):
    batch, seq, d_model = x.shape
    assert d_model == D_MODEL and seq % TOKEN_TILE == 0 and meta.shape == (N_META, D_MODEL)
    assert w_in.shape == (DEPTH, D_MODEL, D_IN)
    tabs = _tables(seq)
    params = (
        pre_norm_g.reshape(DEPTH, 1, D_MODEL),
        b_in.reshape(DEPTH, 1, D_IN),
        conv_w,
        conv_b.reshape(DEPTH, 1, D_CONV),
        ret_gn_g.reshape(DEPTH, 1, D_V),
        post_norm_g.reshape(DEPTH, 1, D_MODEL),
    )
    weights = (w_in, w_conv_out, w_ret_out, w_o)
    meta_h = jnp.zeros((META_ROWS, D_MODEL), F32).at[:N_META].set(meta.astype(F32))
    h = x
    for layer in range(DEPTH):
        h, meta_h = _main_call(h, meta_h, layer, tabs, params, weights)
    return h
```

```python
import functools

import jax
import jax.numpy as jnp
from jax import lax
from jax.experimental import pallas as pl
from jax.experimental.pallas import tpu as pltpu

D_MODEL = 1024
DEPTH = 4
CHUNK = 64
N_META = 16
CONV_WIDTH = 3
D_CONV = D_MODEL
N_HEADS = 4
DK = 256
DV = 512
D_QK = N_HEADS * DK
D_V = N_HEADS * DV
ROPE_BASE = 10000.0
RMS_EPS = 1e-6
GN_EPS = 1e-5
D_IN = 4 * D_CONV + 2 * D_QK + 2 * D_V + 2 * D_MODEL

OFF_CH = 0
OFF_CB = OFF_CH + D_CONV
OFF_CC = OFF_CB + D_CONV
OFF_CZ = OFF_CC + D_CONV
OFF_Q = OFF_CZ + D_CONV
OFF_K = OFF_Q + D_QK
OFF_V = OFF_K + D_QK
OFF_RZ = OFF_V + D_V
OFF_GA = OFF_RZ + D_V
OFF_GB = OFF_GA + D_MODEL

HALF = DK // 2
SUPER = 4 * CHUNK
SUB_TILE = SUPER
TOKEN_TILE = 2 * SUB_TILE
META_ROWS = 128
SUBLANES = 8
VMEM_LIMIT_BYTES = 62 * 1024 * 1024
ROW_PACK = 2
PACKED = jnp.uint32
CAST_ROWS = DK
CAST_COLS = DV

F32 = jnp.float32
BF16 = jnp.bfloat16


def _gammas():
    return [1.0 - 2.0 ** (-5.0 - h) for h in range(N_HEADS)]


def _pack_rows(w):
    return pltpu.bitcast(w.astype(BF16), PACKED)


def _load_weights(layer, weights, stages, sem):
    chunks = [(src, dst, r0, c0) for src, dst in weights
              for r0 in range(0, src.shape[1], CAST_ROWS) for c0 in range(0, src.shape[2], CAST_COLS)]
    depth = len(stages)

    def chunk_copy(j):
        src, _, r0, c0 = chunks[j]
        view = src.at[layer, r0:r0 + CAST_ROWS, c0:c0 + CAST_COLS]
        return pltpu.make_async_copy(view, stages[j % depth], sem.at[j % depth])

    for j in range(min(depth - 1, len(chunks))):
        chunk_copy(j).start()
    for j, (_, dst, r0, c0) in enumerate(chunks):
        chunk_copy(j).wait()
        dst[r0 // ROW_PACK:(r0 + CAST_ROWS) // ROW_PACK, c0:c0 + CAST_COLS] = _pack_rows(stages[j % depth][...])
        if j + depth - 1 < len(chunks):
            chunk_copy(j + depth - 1).start()


def _weight(w_ref, rows, cols):
    packed = w_ref[rows.start // ROW_PACK:rows.stop // ROW_PACK, cols]
    return pltpu.bitcast(packed, BF16)


def _dot(a, b):
    return jnp.dot(a, b, preferred_element_type=F32)


def _dot_nt(a, b):
    return lax.dot_general(a, b, (((1,), (1,)), ((), ())), preferred_element_type=F32)


def _dot_tn(a, b):
    return lax.dot_general(a, b, (((0,), (0,)), ((), ())), preferred_element_type=F32)


def _rmsnorm(x, g):
    ms = jnp.mean(x * x, axis=-1, keepdims=True)
    return x * lax.rsqrt(ms + RMS_EPS) * g


def _silu(x):
    return x * jax.nn.sigmoid(x)


def _rope(x, cos, sin):
    x1, x2 = x[:, :HALF], x[:, HALF:]
    return x1 * cos - x2 * sin, x1 * sin + x2 * cos


def _layer_rows(h, cos, sin, p, u_ref, retention):
    rows = h.shape[0]
    hg = h * p["pre_g"][...]
    rinv = lax.rsqrt(jnp.mean(h * h, axis=-1, keepdims=True) + RMS_EPS)
    xn = (hg * rinv).astype(BF16)
    hg = hg.astype(BF16)

    def proj(off, width, early=False):
        w = _weight(p["w_in"], slice(0, D_MODEL), slice(off, off + width))
        if early:
            return _dot(hg, w) * rinv + p["b_in"][:, off:off + width]
        return _dot(xn, w) + p["b_in"][:, off:off + width]

    def head_proj(hd):
        early = hd == 0
        return (proj(OFF_Q + hd * DK, DK, early), proj(OFF_K + hd * DK, DK, early),
                proj(OFF_V + hd * DV, DV, early), proj(OFF_RZ + hd * DV, DV, early))

    def head_rope(raw):
        q, k, v, z = raw
        return _rope(q, cos, sin), _rope(k, cos, sin), v.astype(BF16), z

    def head_norm(hd, o, z):
        mu = jnp.mean(o, axis=-1, keepdims=True)
        oc = o - mu
        var = jnp.mean(oc * oc, axis=-1, keepdims=True)
        on = oc * lax.rsqrt(var + GN_EPS) * p["gn_g"][:, hd * DV:(hd + 1) * DV]
        return (on * _silu(z)).astype(BF16)

    def head_out(hd, on):
        return _dot(on, _weight(p["w_ro"], slice(hd * DV, (hd + 1) * DV), slice(None)))

    raw = head_proj(0)
    y_b = None
    pending = None
    for hd in range(N_HEADS):
        q, k, v, z = head_rope(raw)
        on = None if pending is None else head_norm(*pending)
        if hd + 1 < N_HEADS:
            raw = head_proj(hd + 1)
        else:
            c_c, c_h = proj(OFF_CC, D_CONV), proj(OFF_CH, D_CONV)
        if on is not None:
            y_hd = head_out(pending[0], on)
            y_b = y_hd if y_b is None else y_b + y_hd
        pending = (hd, retention(hd, q, k, v), z)

    on = head_norm(*pending)
    u = c_c * c_h
    u_ref[SUBLANES:SUBLANES + rows, :] = u
    conv = p["conv_b"][...] + p["conv_w"][2:3, :] * u
    conv = conv + p["conv_w"][1:2, :] * u_ref[SUBLANES - 1:SUBLANES - 1 + rows, :]
    conv = conv + p["conv_w"][0:1, :] * u_ref[SUBLANES - 2:SUBLANES - 2 + rows, :]
    c_b, c_z = proj(OFF_CB, D_CONV), proj(OFF_CZ, D_CONV)
    y_b = y_b + head_out(pending[0], on)
    y = (c_b * conv * _silu(c_z)).astype(BF16)
    g_a, g_b = proj(OFF_GA, D_MODEL), proj(OFF_GB, D_MODEL)
    y_a = _dot(y, _weight(p["w_co"], slice(0, D_CONV), slice(None)))
    mix = jax.nn.sigmoid(g_a) * y_a + jax.nn.sigmoid(g_b) * y_b
    y_o = _dot(mix.astype(BF16), _weight(p["w_o"], slice(0, D_MODEL), slice(None)))
    return h + _rmsnorm(y_o, p["post_g"][...])


def _cat(halves):
    return jnp.concatenate(halves, axis=1)


def _meta_retention(dmat_ref, kdec_ref, s_out_ref):
    def retention(hd, q, k, v):
        kdec = kdec_ref[hd]
        qr = _cat(q).astype(BF16)
        kr = _cat(k).astype(BF16)
        kd = _cat([k[0] * kdec, k[1] * kdec]).astype(BF16)
        s = (_dot_nt(qr, kr) * dmat_ref[hd]).astype(BF16)
        s_out_ref[hd] = _dot_tn(kd, v)
        return _dot(s, v)
    return retention


def _main_kernel(h_ref, cos_ref, sin_ref, dmat_ref, qdec_ref, kdec_ref,
                 meta_ref, cosm_ref, sinm_ref, dmatm_ref, kdecm_ref,
                 pre_g, b_in, conv_w, conv_b, gn_g, post_g,
                 w_in_hbm, w_co_hbm, w_ro_hbm, w_o_hbm,
                 out_ref, meta_out_ref,
                 w_in, w_co, w_ro, w_o, sem,
                 state_ref, u_ref, s_init_ref, utail_ref, um_ref, *, layer):
    rows = h_ref.shape[0]
    p = dict(pre_g=pre_g, b_in=b_in, conv_w=conv_w, conv_b=conv_b, gn_g=gn_g, post_g=post_g,
             w_in=w_in, w_co=w_co, w_ro=w_ro, w_o=w_o)

    @pl.when((pl.program_id(0) == 0) & (pl.program_id(1) == 0))
    def _():
        stages = [ref.at[hd] for ref in (state_ref, s_init_ref) for hd in range(N_HEADS)]
        _load_weights(layer, ((w_in_hbm, w_in), (w_co_hbm, w_co), (w_ro_hbm, w_ro), (w_o_hbm, w_o)),
                      stages, sem)
        um_ref[0:SUBLANES, :] = jnp.zeros((SUBLANES, D_MODEL), F32)
        meta_out_ref[...] = _layer_rows(meta_ref[...], cosm_ref[...], sinm_ref[...], p, um_ref,
                                        _meta_retention(dmatm_ref, kdecm_ref, s_init_ref))
        utail_ref[...] = um_ref[N_META:N_META + SUBLANES, :]

    @pl.when(pl.program_id(1) == 0)
    def _():
        state_ref[...] = s_init_ref[...]
        u_ref[0:SUBLANES, :] = utail_ref[...]

    gammas = _gammas()

    def retention(hd, q, k, v):
        outs = []
        for c in range(v.shape[0] // SUPER):
            r = slice(c * SUPER, (c + 1) * SUPER)
            q1, q2 = q[0][r], q[1][r]
            k1, k2 = k[0][r], k[1][r]
            vc = v[r]
            qdec = qdec_ref[hd]
            kdec = kdec_ref[hd]
            qr = _cat([q1, q2]).astype(BF16)
            kr = _cat([k1, k2]).astype(BF16)
            qd = _cat([q1 * qdec, q2 * qdec]).astype(BF16)
            kd = _cat([k1 * kdec, k2 * kdec]).astype(BF16)
            s = _dot_nt(qr, kr)
            state = state_ref[hd]
            cross = _dot(qd, state.astype(BF16))
            state_ref[hd] = (gammas[hd] ** SUPER) * state + _dot_tn(kd, vc)
            outs.append(_dot((s * dmat_ref[hd]).astype(BF16), vc) + cross)
        return outs[0] if len(outs) == 1 else jnp.concatenate(outs, axis=0)

    for sub in range(rows // SUB_TILE):
        r = slice(sub * SUB_TILE, (sub + 1) * SUB_TILE)
        u_view = u_ref.at[sub * SUB_TILE:(sub + 1) * SUB_TILE + SUBLANES]
        out_ref[r, :] = _layer_rows(h_ref[r, :], cos_ref[r, :], sin_ref[r, :], p, u_view, retention)
    u_ref[0:SUBLANES, :] = u_ref[rows:rows + SUBLANES, :]


def _tables(seq):
    total = N_META + seq
    pos = jnp.arange(total, dtype=F32)
    inv = ROPE_BASE ** (-jnp.arange(HALF, dtype=F32) / HALF)
    ang = pos[:, None] * inv[None, :]
    cos, sin = jnp.cos(ang), jnp.sin(ang)

    log_g = jnp.log(1.0 - 2.0 ** (-5.0 - jnp.arange(N_HEADS, dtype=F32)))[:, None, None]
    scale = DK ** -0.5
    i = jnp.arange(SUPER, dtype=F32)
    dist = jnp.abs(i[:, None] - i[None, :])
    visible = (jnp.arange(SUPER)[None, :] // CHUNK) <= (jnp.arange(SUPER)[:, None] // CHUNK)
    dmat = jnp.where(visible[None], jnp.exp(log_g * dist[None]), 0.0) * scale
    lane = jnp.ones((1, 1, HALF), F32)
    qdec = jnp.exp(log_g * (i[None, :, None] + 1.0)) * lane
    kdec = jnp.exp(log_g * (SUPER - 1.0 - i[None, :, None])) * scale * lane

    m = jnp.arange(META_ROWS, dtype=F32)
    is_meta = jnp.arange(META_ROWS) < N_META
    mdist = jnp.abs(m[:, None] - m[None, :])
    dmat_m = jnp.where((is_meta[:, None] & is_meta[None, :])[None], jnp.exp(log_g * mdist[None]), 0.0) * scale
    kdec_m = jnp.where(is_meta[None, :, None], jnp.exp(log_g * (N_META - 1.0 - m[None, :, None])), 0.0)
    kdec_m = kdec_m * scale * lane
    return dict(cos=cos, sin=sin, dmat=dmat, qdec=qdec, kdec=kdec, dmat_m=dmat_m, kdec_m=kdec_m)


def _resident(block_shape, index_map):
    return pl.BlockSpec(block_shape, index_map, pipeline_mode=pl.Buffered(1))


def _layer_map(b, t, layer, nd):
    return (layer,) + (0,) * nd


def _const_map(b, t, nd):
    return (0,) * nd


def _main_call(h, meta_h, layer, tabs, small_params, weights):
    batch, seq, _ = h.shape
    consts = (meta_h, tabs["cos"][:META_ROWS], tabs["sin"][:META_ROWS], tabs["dmat_m"], tabs["kdec_m"])
    in_specs = [
        pl.BlockSpec((None, TOKEN_TILE, D_MODEL), lambda b, t: (b, t, 0)),
        pl.BlockSpec((TOKEN_TILE, HALF), lambda b, t: (t, 0)),
        pl.BlockSpec((TOKEN_TILE, HALF), lambda b, t: (t, 0)),
    ]
    in_specs += [_resident(a.shape, functools.partial(_const_map, nd=a.ndim))
                 for a in (tabs["dmat"], tabs["qdec"], tabs["kdec"]) + consts]
    in_specs += [_resident((None,) + a.shape[1:], functools.partial(_layer_map, layer=layer, nd=a.ndim - 1))
                 for a in small_params]
    in_specs += [pl.BlockSpec(memory_space=pl.ANY)] * len(weights)
    w_in, w_co, w_ro, w_o = weights
    return pl.pallas_call(
        functools.partial(_main_kernel, layer=layer),
        grid=(batch, seq // TOKEN_TILE),
        in_specs=in_specs,
        out_specs=(pl.BlockSpec((None, TOKEN_TILE, D_MODEL), lambda b, t: (b, t, 0)),
                   pl.BlockSpec((META_ROWS, D_MODEL), functools.partial(_const_map, nd=2))),
        out_shape=(jax.ShapeDtypeStruct(h.shape, F32), jax.ShapeDtypeStruct(meta_h.shape, F32)),
        scratch_shapes=[pltpu.VMEM((w_in.shape[1] // ROW_PACK, w_in.shape[2]), PACKED),
                        pltpu.VMEM((w_co.shape[1] // ROW_PACK, w_co.shape[2]), PACKED),
                        pltpu.VMEM((w_ro.shape[1] // ROW_PACK, w_ro.shape[2]), PACKED),
                        pltpu.VMEM((w_o.shape[1] // ROW_PACK, w_o.shape[2]), PACKED),
                        pltpu.SemaphoreType.DMA((2 * N_HEADS,)),
                        pltpu.VMEM((N_HEADS, DK, DV), F32),
                        pltpu.VMEM((SUBLANES + TOKEN_TILE, D_MODEL), F32),
                        pltpu.VMEM((N_HEADS, DK, DV), F32),
                        pltpu.VMEM((SUBLANES, D_MODEL), F32),
                        pltpu.VMEM((SUBLANES + META_ROWS, D_MODEL), F32)],
        compiler_params=pltpu.CompilerParams(dimension_semantics=("arbitrary", "arbitrary"),
                                             vmem_limit_bytes=VMEM_LIMIT_BYTES),
        name="mixer_layer",
    )(h, tabs["cos"][N_META:], tabs["sin"][N_META:], tabs["dmat"], tabs["qdec"], tabs["kdec"],
      *consts, *small_params, *weights)


def kernel(x, meta, pre_norm_g, w_in, b_in, conv_w, conv_b, w_conv_out, ret_gn_g, w_ret_out, w_o, post_norm_g):
    batch, seq, d_model = x.shape
    assert d_model == D_MODEL and seq % TOKEN_TILE == 0 and meta.shape == (N_META, D_MODEL)
    assert w_in.shape == (DEPTH, D_MODEL, D_IN)
    tabs = _tables(seq)
    params = (
        pre_norm_g.reshape(DEPTH, 1, D_MODEL),
        b_in.reshape(DEPTH, 1, D_IN),
        conv_w,
        conv_b.reshape(DEPTH, 1, D_CONV),
        ret_gn_g.reshape(DEPTH, 1, D_V),
        post_norm_g.reshape(DEPTH, 1, D_MODEL),
    )
    weights = (w_in, w_conv_out, w_ret_out, w_o)
    meta_h = jnp.zeros((META_ROWS, D_MODEL), F32).at[:N_META].set(meta.astype(F32))
    h = x
    for layer in range(DEPTH):
        h, meta_h = _main_call(h, meta_h, layer, tabs, params, weights)
    return h
```

```python
import functools

import jax
import jax.numpy as jnp
from jax import lax
from jax.experimental import pallas as pl
from jax.experimental.pallas import tpu as pltpu

D_MODEL = 1024
DEPTH = 4
CHUNK = 64
N_META = 16
CONV_WIDTH = 3
D_CONV = D_MODEL
N_HEADS = 4
DK = 256
DV = 512
D_QK = N_HEADS * DK
D_V = N_HEADS * DV
ROPE_BASE = 10000.0
RMS_EPS = 1e-6
GN_EPS = 1e-5
D_IN = 4 * D_CONV + 2 * D_QK + 2 * D_V + 2 * D_MODEL

OFF_CH = 0
OFF_CB = OFF_CH + D_CONV
OFF_CC = OFF_CB + D_CONV
OFF_CZ = OFF_CC + D_CONV
OFF_Q = OFF_CZ + D_CONV
OFF_K = OFF_Q + D_QK
OFF_V = OFF_K + D_QK
OFF_RZ = OFF_V + D_V
OFF_GA = OFF_RZ + D_V
OFF_GB = OFF_GA + D_MODEL

HEAD_WIDTH = 2 * DK + 2 * DV
W_IN_GROUPS = tuple(
    piece for hd in range(N_HEADS)
    for piece in ((OFF_Q + hd * DK, DK), (OFF_K + hd * DK, DK), (OFF_V + hd * DV, DV), (OFF_RZ + hd * DV, DV))
) + ((OFF_CC, D_CONV), (OFF_CH, D_CONV), (OFF_CB, D_CONV), (OFF_CZ, D_CONV), (OFF_GA, D_MODEL), (OFF_GB, D_MODEL))
NEW_HEADS = 0
NEW_CONV_IN = N_HEADS * HEAD_WIDTH
NEW_CONV_GATE = NEW_CONV_IN + 2 * D_CONV
NEW_MERGE = NEW_CONV_GATE + 2 * D_CONV

HALF = DK // 2
SUPER = 4 * CHUNK
SUB_TILE = SUPER
TOKEN_TILE = 2 * SUB_TILE
META_ROWS = 128
SUBLANES = 8
VMEM_LIMIT_BYTES = 62 * 1024 * 1024
ROW_PACK = 2
PACKED = jnp.uint32
CAST_ROWS = DK
CAST_COLS = DK

F32 = jnp.float32
BF16 = jnp.bfloat16


def _gammas():
    return [1.0 - 2.0 ** (-5.0 - h) for h in range(N_HEADS)]


def _pack_rows(w):
    return pltpu.bitcast(w.astype(BF16), PACKED)


def _column_map(groups):
    mapping, new = {}, 0
    for off, width in groups:
        for c in range(0, width, CAST_COLS):
            mapping[off + c] = new + c
        new += width
    return mapping


def _load_weights(layer, weights, stages, sem):
    chunks = [(src, dst, r0, c0, c0 if cmap is None else cmap[c0]) for src, dst, cmap in weights
              for r0 in range(0, src.shape[1], CAST_ROWS) for c0 in range(0, src.shape[2], CAST_COLS)]
    depth = len(stages)

    def chunk_copy(j):
        src, _, r0, c0, _ = chunks[j]
        view = src.at[layer, r0:r0 + CAST_ROWS, c0:c0 + CAST_COLS]
        return pltpu.make_async_copy(view, stages[j % depth], sem.at[j % depth])

    for j in range(min(depth - 1, len(chunks))):
        chunk_copy(j).start()
    for j, (_, dst, r0, _, c1) in enumerate(chunks):
        chunk_copy(j).wait()
        dst[r0 // ROW_PACK:(r0 + CAST_ROWS) // ROW_PACK, c1:c1 + CAST_COLS] = _pack_rows(stages[j % depth][...])
        if j + depth - 1 < len(chunks):
            chunk_copy(j + depth - 1).start()


def _weight(w_ref, rows, cols):
    packed = w_ref[rows.start // ROW_PACK:rows.stop // ROW_PACK, cols]
    return pltpu.bitcast(packed, BF16)


def _dot(a, b):
    return jnp.dot(a, b, preferred_element_type=F32)


def _dot_nt(a, b):
    return lax.dot_general(a, b, (((1,), (1,)), ((), ())), preferred_element_type=F32)


def _dot_tn(a, b):
    return lax.dot_general(a, b, (((0,), (0,)), ((), ())), preferred_element_type=F32)


def _rmsnorm(x, g):
    ms = jnp.mean(x * x, axis=-1, keepdims=True)
    return x * lax.rsqrt(ms + RMS_EPS) * g


def _silu(x):
    return x * jax.nn.sigmoid(x)


def _rope(x, cos, sin):
    x1, x2 = x[:, :HALF], x[:, HALF:]
    return x1 * cos - x2 * sin, x1 * sin + x2 * cos


def _layer_rows(h, cos, sin, p, u_ref, retention):
    rows = h.shape[0]
    xn = _rmsnorm(h, p["pre_g"][...]).astype(BF16)

    def proj(off, width):
        w = _weight(p["w_in"], slice(0, D_MODEL), slice(off, off + width))
        return _dot(xn, w) + p["b_in"][:, off:off + width]

    def head_proj(hd):
        r = proj(NEW_HEADS + hd * HEAD_WIDTH, HEAD_WIDTH)
        return r[:, :DK], r[:, DK:2 * DK], r[:, 2 * DK:2 * DK + DV], r[:, 2 * DK + DV:]

    def head_rope(raw):
        q, k, v, z = raw
        return _rope(q, cos, sin), _rope(k, cos, sin), v.astype(BF16), z

    def head_norm(hd, o, z):
        mu = jnp.mean(o, axis=-1, keepdims=True)
        oc = o - mu
        var = jnp.mean(oc * oc, axis=-1, keepdims=True)
        on = oc * lax.rsqrt(var + GN_EPS) * p["gn_g"][:, hd * DV:(hd + 1) * DV]
        return (on * _silu(z)).astype(BF16)

    def head_out(hd, on):
        return _dot(on, _weight(p["w_ro"], slice(hd * DV, (hd + 1) * DV), slice(None)))

    raw = head_proj(0)
    y_b = None
    pending = None
    for hd in range(N_HEADS):
        q, k, v, z = head_rope(raw)
        on = None if pending is None else head_norm(*pending)
        if hd + 1 < N_HEADS:
            raw = head_proj(hd + 1)
        else:
            r = proj(NEW_CONV_IN, 2 * D_CONV)
            c_c, c_h = r[:, :D_CONV], r[:, D_CONV:]
        if on is not None:
            y_hd = head_out(pending[0], on)
            y_b = y_hd if y_b is None else y_b + y_hd
        pending = (hd, retention(hd, q, k, v), z)

    on = head_norm(*pending)
    u = c_c * c_h
    u_ref[SUBLANES:SUBLANES + rows, :] = u
    conv = p["conv_b"][...] + p["conv_w"][2:3, :] * u
    conv = conv + p["conv_w"][1:2, :] * u_ref[SUBLANES - 1:SUBLANES - 1 + rows, :]
    conv = conv + p["conv_w"][0:1, :] * u_ref[SUBLANES - 2:SUBLANES - 2 + rows, :]
    r = proj(NEW_CONV_GATE, 2 * D_CONV)
    c_b, c_z = r[:, :D_CONV], r[:, D_CONV:]
    y_b = y_b + head_out(pending[0], on)
    y = (c_b * conv * _silu(c_z)).astype(BF16)
    r = proj(NEW_MERGE, 2 * D_MODEL)
    g_a, g_b = r[:, :D_MODEL], r[:, D_MODEL:]
    y_a = _dot(y, _weight(p["w_co"], slice(0, D_CONV), slice(None)))
    mix = jax.nn.sigmoid(g_a) * y_a + jax.nn.sigmoid(g_b) * y_b
    y_o = _dot(mix.astype(BF16), _weight(p["w_o"], slice(0, D_MODEL), slice(None)))
    return h + _rmsnorm(y_o, p["post_g"][...])


def _cat(halves):
    return jnp.concatenate(halves, axis=1)


def _meta_retention(dmat_ref, kdec_ref, s_out_ref):
    def retention(hd, q, k, v):
        kdec = kdec_ref[hd]
        qr = _cat(q).astype(BF16)
        kr = _cat(k).astype(BF16)
        kd = _cat([k[0] * kdec, k[1] * kdec]).astype(BF16)
        s = (_dot_nt(qr, kr) * dmat_ref[hd]).astype(BF16)
        s_out_ref[hd] = _dot_tn(kd, v)
        return _dot(s, v)
    return retention


def _main_kernel(h_ref, cos_ref, sin_ref, dmat_ref, qdec_ref, kdec_ref,
                 meta_ref, cosm_ref, sinm_ref, dmatm_ref, kdecm_ref,
                 pre_g, b_in, conv_w, conv_b, gn_g, post_g,
                 w_in_hbm, w_co_hbm, w_ro_hbm, w_o_hbm,
                 out_ref, meta_out_ref,
                 w_in, w_co, w_ro, w_o, sem,
                 state_ref, u_ref, s_init_ref, utail_ref, um_ref, *, layer):
    rows = h_ref.shape[0]
    p = dict(pre_g=pre_g, b_in=b_in, conv_w=conv_w, conv_b=conv_b, gn_g=gn_g, post_g=post_g,
             w_in=w_in, w_co=w_co, w_ro=w_ro, w_o=w_o)

    @pl.when((pl.program_id(0) == 0) & (pl.program_id(1) == 0))
    def _():
        stages = [ref.at[hd, :, c:c + CAST_COLS] for ref in (state_ref, s_init_ref) for hd in range(N_HEADS)
                  for c in range(0, DV, CAST_COLS)]
        _load_weights(layer, ((w_in_hbm, w_in, _column_map(W_IN_GROUPS)), (w_co_hbm, w_co, None),
                              (w_ro_hbm, w_ro, None), (w_o_hbm, w_o, None)), stages, sem)
        um_ref[0:SUBLANES, :] = jnp.zeros((SUBLANES, D_MODEL), F32)
        meta_out_ref[...] = _layer_rows(meta_ref[...], cosm_ref[...], sinm_ref[...], p, um_ref,
                                        _meta_retention(dmatm_ref, kdecm_ref, s_init_ref))
        utail_ref[...] = um_ref[N_META:N_META + SUBLANES, :]

    @pl.when(pl.program_id(1) == 0)
    def _():
        state_ref[...] = s_init_ref[...]
        u_ref[0:SUBLANES, :] = utail_ref[...]

    gammas = _gammas()

    def retention(hd, q, k, v):
        outs = []
        for c in range(v.shape[0] // SUPER):
            r = slice(c * SUPER, (c + 1) * SUPER)
            q1, q2 = q[0][r], q[1][r]
            k1, k2 = k[0][r], k[1][r]
            vc = v[r]
            qdec = qdec_ref[hd]
            kdec = kdec_ref[hd]
            qr = _cat([q1, q2]).astype(BF16)
            kr = _cat([k1, k2]).astype(BF16)
            qd = _cat([q1 * qdec, q2 * qdec]).astype(BF16)
            kd = _cat([k1 * kdec, k2 * kdec]).astype(BF16)
            s = _dot_nt(qr, kr)
            state = state_ref[hd]
            cross = _dot(qd, state.astype(BF16))
            state_ref[hd] = (gammas[hd] ** SUPER) * state + _dot_tn(kd, vc)
            outs.append(_dot((s * dmat_ref[hd]).astype(BF16), vc) + cross)
        return outs[0] if len(outs) == 1 else jnp.concatenate(outs, axis=0)

    for sub in range(rows // SUB_TILE):
        r = slice(sub * SUB_TILE, (sub + 1) * SUB_TILE)
        u_view = u_ref.at[sub * SUB_TILE:(sub + 1) * SUB_TILE + SUBLANES]
        out_ref[r, :] = _layer_rows(h_ref[r, :], cos_ref[r, :], sin_ref[r, :], p, u_view, retention)
    u_ref[0:SUBLANES, :] = u_ref[rows:rows + SUBLANES, :]


def _tables(seq):
    total = N_META + seq
    pos = jnp.arange(total, dtype=F32)
    inv = ROPE_BASE ** (-jnp.arange(HALF, dtype=F32) / HALF)
    ang = pos[:, None] * inv[None, :]
    cos, sin = jnp.cos(ang), jnp.sin(ang)

    log_g = jnp.log(1.0 - 2.0 ** (-5.0 - jnp.arange(N_HEADS, dtype=F32)))[:, None, None]
    scale = DK ** -0.5
    i = jnp.arange(SUPER, dtype=F32)
    dist = jnp.abs(i[:, None] - i[None, :])
    visible = (jnp.arange(SUPER)[None, :] // CHUNK) <= (jnp.arange(SUPER)[:, None] // CHUNK)
    dmat = jnp.where(visible[None], jnp.exp(log_g * dist[None]), 0.0) * scale
    lane = jnp.ones((1, 1, HALF), F32)
    qdec = jnp.exp(log_g * (i[None, :, None] + 1.0)) * lane
    kdec = jnp.exp(log_g * (SUPER - 1.0 - i[None, :, None])) * scale * lane

    m = jnp.arange(META_ROWS, dtype=F32)
    is_meta = jnp.arange(META_ROWS) < N_META
    mdist = jnp.abs(m[:, None] - m[None, :])
    dmat_m = jnp.where((is_meta[:, None] & is_meta[None, :])[None], jnp.exp(log_g * mdist[None]), 0.0) * scale
    kdec_m = jnp.where(is_meta[None, :, None], jnp.exp(log_g * (N_META - 1.0 - m[None, :, None])), 0.0)
    kdec_m = kdec_m * scale * lane
    return dict(cos=cos, sin=sin, dmat=dmat, qdec=qdec, kdec=kdec, dmat_m=dmat_m, kdec_m=kdec_m)


def _resident(block_shape, index_map):
    return pl.BlockSpec(block_shape, index_map, pipeline_mode=pl.Buffered(1))


def _layer_map(b, t, layer, nd):
    return (layer,) + (0,) * nd


def _const_map(b, t, nd):
    return (0,) * nd


def _main_call(h, meta_h, layer, tabs, small_params, weights):
    batch, seq, _ = h.shape
    consts = (meta_h, tabs["cos"][:META_ROWS], tabs["sin"][:META_ROWS], tabs["dmat_m"], tabs["kdec_m"])
    in_specs = [
        pl.BlockSpec((None, TOKEN_TILE, D_MODEL), lambda b, t: (b, t, 0)),
        pl.BlockSpec((TOKEN_TILE, HALF), lambda b, t: (t, 0)),
        pl.BlockSpec((TOKEN_TILE, HALF), lambda b, t: (t, 0)),
    ]
    in_specs += [_resident(a.shape, functools.partial(_const_map, nd=a.ndim))
                 for a in (tabs["dmat"], tabs["qdec"], tabs["kdec"]) + consts]
    in_specs += [_resident((None,) + a.shape[1:], functools.partial(_layer_map, layer=layer, nd=a.ndim - 1))
                 for a in small_params]
    in_specs += [pl.BlockSpec(memory_space=pl.ANY)] * len(weights)
    w_in, w_co, w_ro, w_o = weights
    return pl.pallas_call(
        functools.partial(_main_kernel, layer=layer),
        grid=(batch, seq // TOKEN_TILE),
        in_specs=in_specs,
        out_specs=(pl.BlockSpec((None, TOKEN_TILE, D_MODEL), lambda b, t: (b, t, 0)),
                   pl.BlockSpec((META_ROWS, D_MODEL), functools.partial(_const_map, nd=2))),
        out_shape=(jax.ShapeDtypeStruct(h.shape, F32), jax.ShapeDtypeStruct(meta_h.shape, F32)),
        scratch_shapes=[pltpu.VMEM((w_in.shape[1] // ROW_PACK, w_in.shape[2]), PACKED),
                        pltpu.VMEM((w_co.shape[1] // ROW_PACK, w_co.shape[2]), PACKED),
                        pltpu.VMEM((w_ro.shape[1] // ROW_PACK, w_ro.shape[2]), PACKED),
                        pltpu.VMEM((w_o.shape[1] // ROW_PACK, w_o.shape[2]), PACKED),
                        pltpu.SemaphoreType.DMA((2 * N_HEADS * (DV // CAST_COLS),)),
                        pltpu.VMEM((N_HEADS, DK, DV), F32),
                        pltpu.VMEM((SUBLANES + TOKEN_TILE, D_MODEL), F32),
                        pltpu.VMEM((N_HEADS, DK, DV), F32),
                        pltpu.VMEM((SUBLANES, D_MODEL), F32),
                        pltpu.VMEM((SUBLANES + META_ROWS, D_MODEL), F32)],
        compiler_params=pltpu.CompilerParams(dimension_semantics=("arbitrary", "arbitrary"),
                                             vmem_limit_bytes=VMEM_LIMIT_BYTES),
        name="mixer_layer",
    )(h, tabs["cos"][N_META:], tabs["sin"][N_META:], tabs["dmat"], tabs["qdec"], tabs["kdec"],
      *consts, *small_params, *weights)


def kernel(x, meta, pre_norm_g, w_in, b_in, conv_w, conv_b, w_conv_out, ret_gn_g, w_ret_out, w_o, post_norm_g):
    batch, seq, d_model = x.shape
    assert d_model == D_MODEL and seq % TOKEN_TILE == 0 and meta.shape == (N_META, D_MODEL)
    assert w_in.shape == (DEPTH, D_MODEL, D_IN)
    tabs = _tables(seq)
    params = (
        pre_norm_g.reshape(DEPTH, 1, D_MODEL),
        jnp.concatenate([b_in[:, off:off + width] for off, width in W_IN_GROUPS], axis=1).reshape(DEPTH, 1, D_IN),
        conv_w,
        conv_b.reshape(DEPTH, 1, D_CONV),
        ret_gn_g.reshape(DEPTH, 1, D_V),
        post_norm_g.reshape(DEPTH, 1, D_MODEL),
    )
    weights = (w_in, w_conv_out, w_ret_out, w_o)
    meta_h = jnp.zeros((META_ROWS, D_MODEL), F32).at[:N_META].set(meta.astype(F32))
    h = x
    for layer in range(DEPTH):
        h, meta_h = _main_call(h, meta_h, layer, tabs, params, weights)
    return h
```

```python
import functools

import jax
import jax.numpy as jnp
from jax import lax
from jax.experimental import pallas as pl
from jax.experimental.pallas import tpu as pltpu

D_MODEL = 1024
DEPTH = 4
CHUNK = 64
N_META = 16
CONV_WIDTH = 3
D_CONV = D_MODEL
N_HEADS = 4
DK = 256
DV = 512
D_QK = N_HEADS * DK
D_V = N_HEADS * DV
ROPE_BASE = 10000.0
RMS_EPS = 1e-6
GN_EPS = 1e-5
D_IN = 4 * D_CONV + 2 * D_QK + 2 * D_V + 2 * D_MODEL

OFF_CH = 0
OFF_CB = OFF_CH + D_CONV
OFF_CC = OFF_CB + D_CONV
OFF_CZ = OFF_CC + D_CONV
OFF_Q = OFF_CZ + D_CONV
OFF_K = OFF_Q + D_QK
OFF_V = OFF_K + D_QK
OFF_RZ = OFF_V + D_V
OFF_GA = OFF_RZ + D_V
OFF_GB = OFF_GA + D_MODEL

HALF = DK // 2
SUPER = 4 * CHUNK
SUB_TILE = SUPER
TOKEN_TILE = 2 * SUB_TILE
META_ROWS = 128
SUBLANES = 8
VMEM_LIMIT_BYTES = 62 * 1024 * 1024
ROW_PACK = 2
PACKED = jnp.uint32
CAST_ROWS = DK
CAST_COLS = DV

F32 = jnp.float32
BF16 = jnp.bfloat16


def _gammas():
    return [1.0 - 2.0 ** (-5.0 - h) for h in range(N_HEADS)]


def _pack_rows(w):
    return pltpu.bitcast(w.astype(BF16), PACKED)


def _load_weights(layer, weights, stages, sem):
    chunks = [(src, dst, r0, c0) for src, dst in weights
              for r0 in range(0, src.shape[1], CAST_ROWS) for c0 in range(0, src.shape[2], CAST_COLS)]
    depth = len(stages)

    def chunk_copy(j):
        src, _, r0, c0 = chunks[j]
        view = src.at[layer, r0:r0 + CAST_ROWS, c0:c0 + CAST_COLS]
        return pltpu.make_async_copy(view, stages[j % depth], sem.at[j % depth])

    for j in range(min(depth - 1, len(chunks))):
        chunk_copy(j).start()
    for j, (_, dst, r0, c0) in enumerate(chunks):
        chunk_copy(j).wait()
        dst[r0 // ROW_PACK:(r0 + CAST_ROWS) // ROW_PACK, c0:c0 + CAST_COLS] = _pack_rows(stages[j % depth][...])
        if j + depth - 1 < len(chunks):
            chunk_copy(j + depth - 1).start()


def _weight(w_ref, rows, cols):
    packed = w_ref[rows.start // ROW_PACK:rows.stop // ROW_PACK, cols]
    return pltpu.bitcast(packed, BF16)


def _dot(a, b):
    return jnp.dot(a, b, preferred_element_type=F32)


def _dot_nt(a, b):
    return lax.dot_general(a, b, (((1,), (1,)), ((), ())), preferred_element_type=F32)


def _dot_tn(a, b):
    return lax.dot_general(a, b, (((0,), (0,)), ((), ())), preferred_element_type=F32)


def _rmsnorm(x, g):
    ms = jnp.mean(x * x, axis=-1, keepdims=True)
    return x * lax.rsqrt(ms + RMS_EPS) * g


def _silu(x):
    return x * jax.nn.sigmoid(x)


def _rope(x, cos, sin):
    x1, x2 = x[:, :HALF], x[:, HALF:]
    return x1 * cos - x2 * sin, x1 * sin + x2 * cos


def _layer_rows(h, cos, sin, p, u_ref, retention):
    rows = h.shape[0]
    xn = _rmsnorm(h, p["pre_g"][...]).astype(BF16)

    def proj(off, width):
        w = _weight(p["w_in"], slice(0, D_MODEL), slice(off, off + width))
        return _dot(xn, w) + p["b_in"][:, off:off + width]

    def head_proj(hd):
        return (proj(OFF_Q + hd * DK, DK), proj(OFF_K + hd * DK, DK),
                proj(OFF_V + hd * DV, DV), proj(OFF_RZ + hd * DV, DV))

    def head_rope(raw):
        q, k, v, z = raw
        return _rope(q, cos, sin), _rope(k, cos, sin), v.astype(BF16), z

    def head_norm(hd, o, z):
        mu = jnp.mean(o, axis=-1, keepdims=True)
        oc = o - mu
        var = jnp.mean(oc * oc, axis=-1, keepdims=True)
        on = oc * lax.rsqrt(var + GN_EPS) * p["gn_g"][:, hd * DV:(hd + 1) * DV]
        return (on * _silu(z)).astype(BF16)

    def head_out(hd, on):
        return _dot(on, _weight(p["w_ro"], slice(hd * DV, (hd + 1) * DV), slice(None)))

    raw = head_proj(0)
    y_b = None
    ons = []
    pending = None
    for hd in range(N_HEADS):
        q, k, v, z = head_rope(raw)
        on = None if pending is None else head_norm(*pending)
        if hd + 1 < N_HEADS:
            raw = head_proj(hd + 1)
        else:
            c_c, c_h = proj(OFF_CC, D_CONV), proj(OFF_CH, D_CONV)
        if on is not None:
            ons.append(on)
            if len(ons) == 2:
                y_b = _dot(_cat(ons), _weight(p["w_ro"], slice(0, 2 * DV), slice(None)))
                ons = []
        pending = (hd, retention(hd, q, k, v), z)

    on = _cat(ons + [head_norm(*pending)])
    u = c_c * c_h
    u_ref[SUBLANES:SUBLANES + rows, :] = u
    conv = p["conv_b"][...] + p["conv_w"][2:3, :] * u
    conv = conv + p["conv_w"][1:2, :] * u_ref[SUBLANES - 1:SUBLANES - 1 + rows, :]
    conv = conv + p["conv_w"][0:1, :] * u_ref[SUBLANES - 2:SUBLANES - 2 + rows, :]
    c_b, c_z = proj(OFF_CB, D_CONV), proj(OFF_CZ, D_CONV)
    y_b = y_b + _dot(on, _weight(p["w_ro"], slice(2 * DV, 4 * DV), slice(None)))
    y = (c_b * conv * _silu(c_z)).astype(BF16)
    g_a, g_b = proj(OFF_GA, D_MODEL), proj(OFF_GB, D_MODEL)
    y_a = _dot(y, _weight(p["w_co"], slice(0, D_CONV), slice(None)))
    mix = jax.nn.sigmoid(g_a) * y_a + jax.nn.sigmoid(g_b) * y_b
    y_o = _dot(mix.astype(BF16), _weight(p["w_o"], slice(0, D_MODEL), slice(None)))
    return h + _rmsnorm(y_o, p["post_g"][...])


def _cat(halves):
    return jnp.concatenate(halves, axis=1)


def _meta_retention(dmat_ref, kdec_ref, s_out_ref):
    def retention(hd, q, k, v):
        kdec = kdec_ref[hd]
        qr = _cat(q).astype(BF16)
        kr = _cat(k).astype(BF16)
        kd = _cat([k[0] * kdec, k[1] * kdec]).astype(BF16)
        s = (_dot_nt(qr, kr) * dmat_ref[hd]).astype(BF16)
        s_out_ref[hd] = _dot_tn(kd, v)
        return _dot(s, v)
    return retention


def _main_kernel(h_ref, cos_ref, sin_ref, dmat_ref, qdec_ref, kdec_ref,
                 meta_ref, cosm_ref, sinm_ref, dmatm_ref, kdecm_ref,
                 pre_g, b_in, conv_w, conv_b, gn_g, post_g,
                 w_in_hbm, w_co_hbm, w_ro_hbm, w_o_hbm,
                 out_ref, meta_out_ref,
                 w_in, w_co, w_ro, w_o, sem,
                 state_ref, u_ref, s_init_ref, utail_ref, um_ref, *, layer):
    rows = h_ref.shape[0]
    p = dict(pre_g=pre_g, b_in=b_in, conv_w=conv_w, conv_b=conv_b, gn_g=gn_g, post_g=post_g,
             w_in=w_in, w_co=w_co, w_ro=w_ro, w_o=w_o)

    @pl.when((pl.program_id(0) == 0) & (pl.program_id(1) == 0))
    def _():
        stages = [ref.at[hd] for ref in (state_ref, s_init_ref) for hd in range(N_HEADS)]
        _load_weights(layer, ((w_in_hbm, w_in), (w_co_hbm, w_co), (w_ro_hbm, w_ro), (w_o_hbm, w_o)),
                      stages, sem)
        um_ref[0:SUBLANES, :] = jnp.zeros((SUBLANES, D_MODEL), F32)
        meta_out_ref[...] = _layer_rows(meta_ref[...], cosm_ref[...], sinm_ref[...], p, um_ref,
                                        _meta_retention(dmatm_ref, kdecm_ref, s_init_ref))
        utail_ref[...] = um_ref[N_META:N_META + SUBLANES, :]

    @pl.when(pl.program_id(1) == 0)
    def _():
        state_ref[...] = s_init_ref[...]
        u_ref[0:SUBLANES, :] = utail_ref[...]

    gammas = _gammas()

    def retention(hd, q, k, v):
        outs = []
        for c in range(v.shape[0] // SUPER):
            r = slice(c * SUPER, (c + 1) * SUPER)
            q1, q2 = q[0][r], q[1][r]
            k1, k2 = k[0][r], k[1][r]
            vc = v[r]
            qdec = qdec_ref[hd]
            kdec = kdec_ref[hd]
            qr = _cat([q1, q2]).astype(BF16)
            kr = _cat([k1, k2]).astype(BF16)
            qd = _cat([q1 * qdec, q2 * qdec]).astype(BF16)
            kd = _cat([k1 * kdec, k2 * kdec]).astype(BF16)
            s = _dot_nt(qr, kr)
            state = state_ref[hd]
            cross = _dot(qd, state.astype(BF16))
            state_ref[hd] = (gammas[hd] ** SUPER) * state + _dot_tn(kd, vc)
            outs.append(_dot((s * dmat_ref[hd]).astype(BF16), vc) + cross)
        return outs[0] if len(outs) == 1 else jnp.concatenate(outs, axis=0)

    for sub in range(rows // SUB_TILE):
        r = slice(sub * SUB_TILE, (sub + 1) * SUB_TILE)
        u_view = u_ref.at[sub * SUB_TILE:(sub + 1) * SUB_TILE + SUBLANES]
        out_ref[r, :] = _layer_rows(h_ref[r, :], cos_ref[r, :], sin_ref[r, :], p, u_view, retention)
    u_ref[0:SUBLANES, :] = u_ref[rows:rows + SUBLANES, :]


def _tables(seq):
    total = N_META + seq
    pos = jnp.arange(total, dtype=F32)
    inv = ROPE_BASE ** (-jnp.arange(HALF, dtype=F32) / HALF)
    ang = pos[:, None] * inv[None, :]
    cos, sin = jnp.cos(ang), jnp.sin(ang)

    log_g = jnp.log(1.0 - 2.0 ** (-5.0 - jnp.arange(N_HEADS, dtype=F32)))[:, None, None]
    scale = DK ** -0.5
    i = jnp.arange(SUPER, dtype=F32)
    dist = jnp.abs(i[:, None] - i[None, :])
    visible = (jnp.arange(SUPER)[None, :] // CHUNK) <= (jnp.arange(SUPER)[:, None] // CHUNK)
    dmat = jnp.where(visible[None], jnp.exp(log_g * dist[None]), 0.0) * scale
    lane = jnp.ones((1, 1, HALF), F32)
    qdec = jnp.exp(log_g * (i[None, :, None] + 1.0)) * lane
    kdec = jnp.exp(log_g * (SUPER - 1.0 - i[None, :, None])) * scale * lane

    m = jnp.arange(META_ROWS, dtype=F32)
    is_meta = jnp.arange(META_ROWS) < N_META
    mdist = jnp.abs(m[:, None] - m[None, :])
    dmat_m = jnp.where((is_meta[:, None] & is_meta[None, :])[None], jnp.exp(log_g * mdist[None]), 0.0) * scale
    kdec_m = jnp.where(is_meta[None, :, None], jnp.exp(log_g * (N_META - 1.0 - m[None, :, None])), 0.0)
    kdec_m = kdec_m * scale * lane
    return dict(cos=cos, sin=sin, dmat=dmat, qdec=qdec, kdec=kdec, dmat_m=dmat_m, kdec_m=kdec_m)


def _resident(block_shape, index_map):
    return pl.BlockSpec(block_shape, index_map, pipeline_mode=pl.Buffered(1))


def _layer_map(b, t, layer, nd):
    return (layer,) + (0,) * nd


def _const_map(b, t, nd):
    return (0,) * nd


def _main_call(h, meta_h, layer, tabs, small_params, weights):
    batch, seq, _ = h.shape
    consts = (meta_h, tabs["cos"][:META_ROWS], tabs["sin"][:META_ROWS], tabs["dmat_m"], tabs["kdec_m"])
    in_specs = [
        pl.BlockSpec((None, TOKEN_TILE, D_MODEL), lambda b, t: (b, t, 0)),
        pl.BlockSpec((TOKEN_TILE, HALF), lambda b, t: (t, 0)),
        pl.BlockSpec((TOKEN_TILE, HALF), lambda b, t: (t, 0)),
    ]
    in_specs += [_resident(a.shape, functools.partial(_const_map, nd=a.ndim))
                 for a in (tabs["dmat"], tabs["qdec"], tabs["kdec"]) + consts]
    in_specs += [_resident((None,) + a.shape[1:], functools.partial(_layer_map, layer=layer, nd=a.ndim - 1))
                 for a in small_params]
    in_specs += [pl.BlockSpec(memory_space=pl.ANY)] * len(weights)
    w_in, w_co, w_ro, w_o = weights
    return pl.pallas_call(
        functools.partial(_main_kernel, layer=layer),
        grid=(batch, seq // TOKEN_TILE),
        in_specs=in_specs,
        out_specs=(pl.BlockSpec((None, TOKEN_TILE, D_MODEL), lambda b, t: (b, t, 0)),
                   pl.BlockSpec((META_ROWS, D_MODEL), functools.partial(_const_map, nd=2))),
        out_shape=(jax.ShapeDtypeStruct(h.shape, F32), jax.ShapeDtypeStruct(meta_h.shape, F32)),
        scratch_shapes=[pltpu.VMEM((w_in.shape[1] // ROW_PACK, w_in.shape[2]), PACKED),
                        pltpu.VMEM((w_co.shape[1] // ROW_PACK, w_co.shape[2]), PACKED),
                        pltpu.VMEM((w_ro.shape[1] // ROW_PACK, w_ro.shape[2]), PACKED),
                        pltpu.VMEM((w_o.shape[1] // ROW_PACK, w_o.shape[2]), PACKED),
                        pltpu.SemaphoreType.DMA((2 * N_HEADS,)),
                        pltpu.VMEM((N_HEADS, DK, DV), F32),
                        pltpu.VMEM((SUBLANES + TOKEN_TILE, D_MODEL), F32),
                        pltpu.VMEM((N_HEADS, DK, DV), F32),
                        pltpu.VMEM((SUBLANES, D_MODEL), F32),
                        pltpu.VMEM((SUBLANES + META_ROWS, D_MODEL), F32)],
        compiler_params=pltpu.CompilerParams(dimension_semantics=("arbitrary", "arbitrary"),
                                             vmem_limit_bytes=VMEM_LIMIT_BYTES),
        name="mixer_layer",
    )(h, tabs["cos"][N_META:], tabs["sin"][N_META:], tabs["dmat"], tabs["qdec"], tabs["kdec"],
      *consts, *small_params, *weights)


def kernel(x, meta, pre_norm_g, w_in, b_in, conv_w, conv_b, w_conv_out, ret_gn_g, w_ret_out, w_o, post_norm_g):
    batch, seq, d_model = x.shape
    assert d_model == D_MODEL and seq % TOKEN_TILE == 0 and meta.shape == (N_META, D_MODEL)
    assert w_in.shape == (DEPTH, D_MODEL, D_IN)
    tabs = _tables(seq)
    params = (
        pre_norm_g.reshape(DEPTH, 1, D_MODEL),
        b_in.reshape(DEPTH, 1, D_IN),
        conv_w,
        conv_b.reshape(DEPTH, 1, D_CONV),
        ret_gn_g.reshape(DEPTH, 1, D_V),
        post_norm_g.reshape(DEPTH, 1, D_MODEL),
    )
    weights = (w_in, w_conv_out, w_ret_out, w_o)
    meta_h = jnp.zeros((META_ROWS, D_MODEL), F32).at[:N_META].set(meta.astype(F32))
    h = x
    for layer in range(DEPTH):
        h, meta_h = _main_call(h, meta_h, layer, tabs, params, weights)
    return h
```
